```python
import jax, jax.numpy as jnp
from jax import lax
import numpy as np

D_MODEL = 1024
BATCH = 8
SEQ = 2048
DEPTH = 4
DEC_BATCH = 128
DEC_SEQ = 4
PAST_LEN = 16384
PAGE_SIZE = 128

GLA_HEADS = 4
GLA_DK = D_MODEL // 16
GLA_DV = D_MODEL // 8
GLA_QK = GLA_HEADS * GLA_DK
GLA_V = GLA_HEADS * GLA_DV
GLA_RANK = 16
GLA_TAU = 16.0
GLA_CHUNK = 64
M_HEADS = 4
M_DH = D_MODEL // 8
M_WIDTH = M_HEADS * M_DH
M_CHUNK = 64
CONV_W = 4
FORGET_BIAS = 3.0
R_WIDTH = D_MODEL // 2
R_BLOCKS = 8
R_BS = R_WIDTH // R_BLOCKS
R_C = 8.0
D_FF = 2816
FFN_CONV_W = 3
N_BRANCH = 3
EPS = 1e-6

IN_SIZES = (GLA_QK, GLA_QK, GLA_V, GLA_V, GLA_RANK,
            M_WIDTH, M_HEADS, M_HEADS, M_WIDTH,
            R_WIDTH, R_WIDTH,
            N_BRANCH * D_MODEL)
D_IN = sum(IN_SIZES)
D_BRANCH = GLA_V + M_WIDTH + R_WIDTH

kernel_name = 'hybrid_gla_mlstm_rglru_convffn_step'


def _offsets(sizes):
    return [int(o) for o in np.cumsum(sizes)[:-1]]


def rmsnorm(x, g):
    xf = x.astype(jnp.float32)
    y = xf * lax.rsqrt(jnp.mean(xf * xf, axis=-1, keepdims=True) + EPS)
    return (y * g.astype(jnp.float32)).astype(x.dtype)


def head_rmsnorm(h, g, n_heads):
    hs = h.reshape(h.shape[:-1] + (n_heads, -1))
    hs = hs * lax.rsqrt(jnp.mean(hs * hs, axis=-1, keepdims=True) + EPS)
    return hs.reshape(h.shape) * g.astype(jnp.float32)


def blockdiag(x, w):
    nb, bs, _ = w.shape
    xb = x.reshape(x.shape[:-1] + (nb, bs))
    return jnp.einsum('blnd,nde->blne', xb, w.astype(jnp.float32)).reshape(x.shape)


def causal_dwconv(u, buf, w, b):
    width = w.shape[0]
    L = u.shape[1]
    ucat = jnp.concatenate([buf.astype(u.dtype), u], axis=1)
    w = w.astype(u.dtype)
    y = b.astype(u.dtype)
    for j in range(width):
        y = y + ucat[:, j:j + L] * w[j]
    return y, ucat[:, L:]


def _pad_time(a, n_pad, value):
    if n_pad == 0:
        return a
    pad = [(0, 0)] * a.ndim
    pad[1] = (0, n_pad)
    return jnp.pad(a, pad, constant_values=value)


def gla_chunked(q, k, v, lg, S0):
    B, L, H, DK = q.shape
    DV = v.shape[-1]
    c = min(GLA_CHUNK, L)
    n_pad = (-L) % c
    q, k, v, lg = (_pad_time(a, n_pad, 0.0) for a in (q, k, v, lg))
    nC = (L + n_pad) // c
    q, k, v, lg = (a.reshape(B, nC, c, H, a.shape[-1]) for a in (q, k, v, lg))
    bcum = jnp.cumsum(lg, axis=2)
    blast = bcum[:, :, -1:]
    q_d = q * jnp.exp(bcum)
    k_d = k * jnp.exp(-bcum)
    k_e = k * jnp.exp(blast - bcum)
    mask = jnp.tril(jnp.ones((c, c), bool))
    A = jnp.where(mask, jnp.einsum('bnihd,bnjhd->bnhij', q_d, k_d), 0.0)
    o_intra = jnp.einsum('bnhij,bnjhe->bnihe', A, v)
    dS = jnp.einsum('bnjhd,bnjhe->bnhde', k_e, v)
    decay = jnp.exp(blast[:, :, 0])

    def step(S, inp):
        dec, ds = inp
        return dec[..., None] * S + ds, S

    S_last, S_in = lax.scan(step, S0, (jnp.moveaxis(decay, 1, 0), jnp.moveaxis(dS, 1, 0)))
    S_in = jnp.moveaxis(S_in, 0, 1)
    o = o_intra + jnp.einsum('bnihd,bnhde->bnihe', q_d, S_in)
    return o.reshape(B, nC * c, H, DV)[:, :L], S_last


def mlstm_chunked(q, k, v, ig, lf, C0, n0, m0):
    B, L, H, D = q.shape
    c = min(M_CHUNK, L)
    n_pad = (-L) % c
    q, k, v, lf = (_pad_time(a, n_pad, 0.0) for a in (q, k, v, lf))
    ig = _pad_time(ig, n_pad, -jnp.inf)
    nC = (L + n_pad) // c
    chunks = lambda a: jnp.moveaxis(a.reshape((B, nC, c) + a.shape[2:]), 1, 0)
    Fcum = jnp.cumsum(lf.reshape(B, nC, c, H), axis=2)
    xs = (chunks(q), chunks(k), chunks(v), chunks(ig), jnp.moveaxis(Fcum, 1, 0))
    mask = jnp.tril(jnp.ones((c, c), bool))

    def step(carry, inp):
        C, n, m = carry
        qc, kc, vc, ic, Fc = inp
        Ft = jnp.swapaxes(Fc, 1, 2)
        it = jnp.swapaxes(ic, 1, 2)
        Dm = jnp.where(mask, Ft[..., :, None] - Ft[..., None, :] + it[..., None, :], -jnp.inf)
        b = Ft + m[..., None]
        mt = jnp.maximum(b, jnp.max(Dm, axis=-1))
        w_inter = jnp.exp(b - mt)
        P = jnp.exp(Dm - mt[..., None]) * jnp.einsum('bthd,bshd->bhts', qc, kc)
        num = (w_inter[..., None] * jnp.einsum('bthk,bhvk->bhtv', qc, C)
               + jnp.einsum('bhts,bshv->bhtv', P, vc))
        den = w_inter * jnp.einsum('bthk,bhk->bht', qc, n) + jnp.sum(P, axis=-1)
        h = num / jnp.maximum(jnp.abs(den), jnp.exp(-mt))[..., None]
        m_new = mt[..., -1]
        dec = jnp.exp(b[..., -1] - m_new)
        ws = jnp.exp(Dm[..., -1, :] - m_new[..., None])
        C_new = dec[..., None, None] * C + jnp.einsum('bhs,bshv,bshk->bhvk', ws, vc, kc)
        n_new = dec[..., None] * n + jnp.einsum('bhs,bshk->bhk', ws, kc)
        return (C_new, n_new, m_new), jnp.swapaxes(h, 1, 2)

    (C, n, m), h = lax.scan(step, (C0, n0, m0), xs)
    h = jnp.moveaxis(h, 0, 1).reshape(B, nC * c, H, D)[:, :L]
    return h, C, n, m


def rglru(xc, r, i, lam, h0):
    log_a = R_C * r * jax.nn.log_sigmoid(lam.astype(jnp.float32))
    a = jnp.exp(log_a)
    bx = jnp.sqrt(-jnp.expm1(2.0 * log_a)) * (i * xc)

    def comb(lhs, rhs):
        a1, b1 = lhs
        a2, b2 = rhs
        return a1 * a2, a2 * b1 + b2

    A, Bc = lax.associative_scan(comb, (a, bx), axis=1)
    h = Bc + A * h0[:, None]
    return h, h[:, -1]


def _layer(x, st, p):
    gS, mC, mn, mm, mconv, rh, rconv, fconv = st
    f32 = jnp.float32
    dt = x.dtype
    B, L, _ = x.shape

    xn = rmsnorm(x, p['norm_mix_g'])
    proj = (xn @ p['w_in'] + p['b_in']).astype(f32)
    gq, gk, gv, gg, ga, mu, mi, mf, mo, rx, ry, gates = jnp.split(proj, _offsets(IN_SIZES), axis=-1)

    q = gq.reshape(B, L, GLA_HEADS, GLA_DK) * (GLA_DK ** -0.5)
    k = gk.reshape(B, L, GLA_HEADS, GLA_DK)
    v = gv.reshape(B, L, GLA_HEADS, GLA_DV)
    lg = (jax.nn.log_sigmoid(ga @ p['w_gla_a2'].astype(f32) + p['b_gla_a2'].astype(f32))
          / GLA_TAU).reshape(B, L, GLA_HEADS, GLA_DK)
    o, gS_new = gla_chunked(q, k, v, lg, gS.astype(f32))
    o_gla = head_rmsnorm(o.reshape(B, L, GLA_V), p['gla_norm_g'], GLA_HEADS) * jax.nn.silu(gg)

    mcv, mconv_new = causal_dwconv(mu, mconv.astype(f32), p['mlstm_conv_w'].astype(f32), p['mlstm_conv_b'].astype(f32))
    mcv = jax.nn.silu(mcv)
    mq = blockdiag(mcv, p['w_mlstm_q']).reshape(B, L, M_HEADS, M_DH)
    mk = (blockdiag(mcv, p['w_mlstm_k']) * (M_DH ** -0.5)).reshape(B, L, M_HEADS, M_DH)
    mvv = blockdiag(mu, p['w_mlstm_v']).reshape(B, L, M_HEADS, M_DH)
    hm, mC_new, mn_new, mm_new = mlstm_chunked(mq, mk, mvv, mi, jax.nn.log_sigmoid(mf),
                                               mC.astype(f32), mn.astype(f32), mm.astype(f32))
    o_m = head_rmsnorm(hm.reshape(B, L, M_WIDTH), p['mlstm_norm_g'], M_HEADS) * jax.nn.sigmoid(mo)

    xc, rconv_new = causal_dwconv(rx, rconv.astype(f32), p['rglru_conv_w'].astype(f32), p['rglru_conv_b'].astype(f32))
    r_g = jax.nn.sigmoid(blockdiag(xc, p['w_rglru_r']) + p['b_rglru_r'].astype(f32))
    i_g = jax.nn.sigmoid(blockdiag(xc, p['w_rglru_i']) + p['b_rglru_i'].astype(f32))
    hr, rh_new = rglru(xc, r_g, i_g, p['rglru_lambda'], rh.astype(f32))
    o_r = hr * jax.nn.gelu(ry)

    wb = p['w_branch'].astype(f32)
    g3 = jax.nn.sigmoid(gates).reshape(B, L, N_BRANCH, D_MODEL)
    mix = (g3[:, :, 0] * (o_gla @ wb[:GLA_V])
           + g3[:, :, 1] * (o_m @ wb[GLA_V:GLA_V + M_WIDTH])
           + g3[:, :, 2] * (o_r @ wb[GLA_V + M_WIDTH:]))
    x = (x.astype(f32) + mix @ p['w_out'].astype(f32)).astype(dt)

    xn2 = rmsnorm(x, p['norm_ffn_g'])
    u = (xn2 @ p['w_up']).astype(f32)
    uc, fconv_new = causal_dwconv(u, fconv.astype(f32), p['ffn_conv_w'].astype(f32), p['ffn_conv_b'].astype(f32))
    u_a, u_b = jnp.split(uc, 2, axis=-1)
    x = (x.astype(f32) + (jax.nn.gelu(u_a) * u_b) @ p['w_down'].astype(f32)).astype(dt)

    new_st = [gS_new, mC_new, mn_new, mm_new, mconv_new, rh_new, rconv_new, fconv_new]
    return x, [s.astype(dt) for s in new_st]


def _trunk(x, states, weights, norm_f_g):
    per_layer = []
    for l in range(DEPTH):
        p = {name: w[l] for name, w in weights.items()}
        x, st = _layer(x, [s[l] for s in states], p)
        per_layer.append(st)
    new_states = [jnp.stack([st[j] for st in per_layer]) for j in range(len(states))]
    return rmsnorm(x, norm_f_g), new_states


def setup_inputs(seed: int = 0) -> dict:
    key = jax.random.key(seed)
    ks = jax.random.split(key, 40)
    f32 = jnp.float32

    def nrm(i, shape, s):
        return s * jax.random.normal(ks[i], shape, f32)

    b_in = nrm(12, (DEPTH, D_IN), 0.01)
    f_off = sum(IN_SIZES[:7])
    b_in = b_in.at[:, f_off:f_off + M_HEADS].add(FORGET_BIAS)
    u_lam = jax.random.uniform(ks[30], (DEPTH, R_WIDTH), f32, 0.9, 0.999)
    s_lam = u_lam ** (1.0 / R_C)
    rglru_lambda = jnp.log(s_lam) - jnp.log1p(-s_lam)
    return {
        'x_prompt': nrm(0, (BATCH, SEQ, D_MODEL), 1.0),
        'x_sample': nrm(1, (DEC_BATCH, DEC_SEQ, D_MODEL), 1.0),
        'state_gla_S': nrm(2, (DEPTH, DEC_BATCH, GLA_HEADS, GLA_DK, GLA_DV), 0.5),
        'state_mlstm_C': nrm(3, (DEPTH, DEC_BATCH, M_HEADS, M_DH, M_DH), 0.1),
        'state_mlstm_n': nrm(4, (DEPTH, DEC_BATCH, M_HEADS, M_DH), 0.1),
        'state_mlstm_m': nrm(5, (DEPTH, DEC_BATCH, M_HEADS), 1.0),
        'state_mlstm_conv': nrm(6, (DEPTH, DEC_BATCH, CONV_W - 1, M_WIDTH), 1.0),
        'state_rglru_h': nrm(7, (DEPTH, DEC_BATCH, R_WIDTH), 0.5),
        'state_rglru_conv': nrm(8, (DEPTH, DEC_BATCH, CONV_W - 1, R_WIDTH), 1.0),
        'state_ffn_conv': nrm(9, (DEPTH, DEC_BATCH, FFN_CONV_W - 1, 2 * D_FF), 1.0),
        'norm_mix_g': 1.0 + nrm(10, (DEPTH, D_MODEL), 0.05),
        'w_in': nrm(11, (DEPTH, D_MODEL, D_IN), D_MODEL ** -0.5),
        'b_in': b_in,
        'w_gla_a2': nrm(13, (DEPTH, GLA_RANK, GLA_QK), GLA_RANK ** -0.5),
        'b_gla_a2': nrm(14, (DEPTH, GLA_QK), 0.01),
        'gla_norm_g': 1.0 + nrm(15, (DEPTH, GLA_V), 0.05),
        'mlstm_conv_w': nrm(16, (DEPTH, CONV_W, M_WIDTH), CONV_W ** -0.5),
        'mlstm_conv_b': nrm(17, (DEPTH, M_WIDTH), 0.01),
        'w_mlstm_q': nrm(18, (DEPTH, M_HEADS, M_DH, M_DH), M_DH ** -0.5),
        'w_mlstm_k': nrm(19, (DEPTH, M_HEADS, M_DH, M_DH), M_DH ** -0.5),
        'w_mlstm_v': nrm(20, (DEPTH, M_HEADS, M_DH, M_DH), M_DH ** -0.5),
        'mlstm_norm_g': 1.0 + nrm(21, (DEPTH, M_WIDTH), 0.05),
        'rglru_conv_w': nrm(22, (DEPTH, CONV_W, R_WIDTH), CONV_W ** -0.5),
        'rglru_conv_b': nrm(23, (DEPTH, R_WIDTH), 0.01),
        'w_rglru_r': nrm(24, (DEPTH, R_BLOCKS, R_BS, R_BS), R_BS ** -0.5),
        'b_rglru_r': nrm(25, (DEPTH, R_WIDTH), 0.01),
        'w_rglru_i': nrm(26, (DEPTH, R_BLOCKS, R_BS, R_BS), R_BS ** -0.5),
        'b_rglru_i': nrm(27, (DEPTH, R_WIDTH), 0.01),
        'rglru_lambda': rglru_lambda,
        'w_branch': nrm(28, (DEPTH, D_BRANCH, D_MODEL), GLA_V ** -0.5),
        'w_out': nrm(29, (DEPTH, D_MODEL, D_MODEL), D_MODEL ** -0.5),
        'norm_ffn_g': 1.0 + nrm(31, (DEPTH, D_MODEL), 0.05),
        'w_up': nrm(32, (DEPTH, D_MODEL, 2 * D_FF), D_MODEL ** -0.5),
        'ffn_conv_w': nrm(33, (DEPTH, FFN_CONV_W, 2 * D_FF), FFN_CONV_W ** -0.5),
        'ffn_conv_b': nrm(34, (DEPTH, 2 * D_FF), 0.01),
        'w_down': nrm(35, (DEPTH, D_FF, D_MODEL), D_FF ** -0.5),
        'norm_f_g': 1.0 + nrm(36, (D_MODEL,), 0.05),
    }


def reference(x_prompt, x_sample, state_gla_S, state_mlstm_C, state_mlstm_n, state_mlstm_m,
              state_mlstm_conv, state_rglru_h, state_rglru_conv, state_ffn_conv,
              norm_mix_g, w_in, b_in, w_gla_a2, b_gla_a2, gla_norm_g,
              mlstm_conv_w, mlstm_conv_b, w_mlstm_q, w_mlstm_k, w_mlstm_v, mlstm_norm_g,
              rglru_conv_w, rglru_conv_b, w_rglru_r, b_rglru_r, w_rglru_i, b_rglru_i, rglru_lambda,
              w_branch, w_out, norm_ffn_g, w_up, ffn_conv_w, ffn_conv_b, w_down, norm_f_g):
    weights = {
        'norm_mix_g': norm_mix_g, 'w_in': w_in, 'b_in': b_in,
        'w_gla_a2': w_gla_a2, 'b_gla_a2': b_gla_a2, 'gla_norm_g': gla_norm_g,
        'mlstm_conv_w': mlstm_conv_w, 'mlstm_conv_b': mlstm_conv_b,
        'w_mlstm_q': w_mlstm_q, 'w_mlstm_k': w_mlstm_k, 'w_mlstm_v': w_mlstm_v,
        'mlstm_norm_g': mlstm_norm_g,
        'rglru_conv_w': rglru_conv_w, 'rglru_conv_b': rglru_conv_b,
        'w_rglru_r': w_rglru_r, 'b_rglru_r': b_rglru_r,
        'w_rglru_i': w_rglru_i, 'b_rglru_i': b_rglru_i, 'rglru_lambda': rglru_lambda,
        'w_branch': w_branch, 'w_out': w_out, 'norm_ffn_g': norm_ffn_g,
        'w_up': w_up, 'ffn_conv_w': ffn_conv_w, 'ffn_conv_b': ffn_conv_b, 'w_down': w_down,
    }
    bp = x_prompt.shape[0]
    dt = x_prompt.dtype
    zero_states = [
        jnp.zeros((DEPTH, bp, GLA_HEADS, GLA_DK, GLA_DV), dt),
        jnp.zeros((DEPTH, bp, M_HEADS, M_DH, M_DH), dt),
        jnp.zeros((DEPTH, bp, M_HEADS, M_DH), dt),
        jnp.zeros((DEPTH, bp, M_HEADS), dt),
        jnp.zeros((DEPTH, bp, CONV_W - 1, M_WIDTH), dt),
        jnp.zeros((DEPTH, bp, R_WIDTH), dt),
        jnp.zeros((DEPTH, bp, CONV_W - 1, R_WIDTH), dt),
        jnp.zeros((DEPTH, bp, FFN_CONV_W - 1, 2 * D_FF), dt),
    ]
    y_prompt, p_states = _trunk(x_prompt, zero_states, weights, norm_f_g)
    sample_states = [state_gla_S, state_mlstm_C, state_mlstm_n, state_mlstm_m,
                     state_mlstm_conv, state_rglru_h, state_rglru_conv, state_ffn_conv]
    y_sample, s_states = _trunk(x_sample, sample_states, weights, norm_f_g)
    p_gla_S, p_mlstm_C, p_mlstm_n, p_mlstm_m, p_mlstm_conv, p_rglru_h, p_rglru_conv, p_ffn_conv = p_states
    s_gla_S, s_mlstm_C, s_mlstm_n, s_mlstm_m, s_mlstm_conv, s_rglru_h, s_rglru_conv, s_ffn_conv = s_states
    return (y_prompt, y_sample,
            p_gla_S, p_mlstm_C, p_mlstm_n, p_mlstm_m, p_mlstm_conv, p_rglru_h, p_rglru_conv, p_ffn_conv,
            s_gla_S, s_mlstm_C, s_mlstm_n, s_mlstm_m, s_mlstm_conv, s_rglru_h, s_rglru_conv, s_ffn_conv)
```

```python
import functools

import jax
import jax.numpy as jnp
from jax import lax
from jax.experimental import pallas as pl
from jax.experimental.pallas import tpu as pltpu

f32 = jnp.float32
bf16 = jnp.bfloat16

D_MODEL = 1024
DEPTH = 4
GLA_HEADS = 4
GLA_DK = 64
GLA_DV = 128
GLA_RANK = 16
GLA_TAU = 16.0
M_HEADS = 4
M_DH = 128
M_WIDTH = 512
CONV_W = 4
R_WIDTH = 512
R_BLOCKS = 8
R_BS = 64
R_C = 8.0
D_FF = 2816
FFN_CONV_W = 3
EPS = 1e-6

LANES = 128
SUBLANES = 8
CHUNK = 64
VMEM_LIMIT = 52 * 1024 * 1024

COL_GATES = 0
COL_GQ = 3072
COL_GK = 3584
COL_GV = 4096
COL_GG = 4608
COL_MU = 5120
COL_MO = 5632
COL_RX = 6144
COL_RY = 6656
COL_GA = 7168
COL_MIF = 7296
D_IN_PAD = 7680


def _log_sigmoid(x):
    return jnp.minimum(x, 0.0) - jnp.log1p(jnp.exp(-jnp.abs(x)))


def _silu(x):
    return x * jax.nn.sigmoid(x)


def _gelu_tanh(x):
    return 0.5 * x * (1.0 + jnp.tanh(0.7978845608028654 * (x + 0.044715 * (x * x * x))))


def _head_rmsnorm(o, g):
    return o * lax.rsqrt(jnp.mean(o * o, axis=-1, keepdims=True) + EPS) * g


def _dot(a, b):
    return jnp.dot(a.astype(bf16), b.astype(bf16), preferred_element_type=f32)


def _dot_nt(a, b):
    return lax.dot_general(a.astype(bf16), b.astype(bf16), (((1,), (1,)), ((), ())),
                           preferred_element_type=f32)


def _dot_tn(a, b):
    return lax.dot_general(a.astype(bf16), b.astype(bf16), (((0,), (0,)), ((), ())),
                           preferred_element_type=f32)


def _cumsum_rows(tri, x):
    return jnp.dot(tri, x, preferred_element_type=f32, precision=lax.Precision.HIGHEST)


def _tri(c):
    r = lax.broadcasted_iota(jnp.int32, (c, c), 0)
    s = lax.broadcasted_iota(jnp.int32, (c, c), 1)
    return r >= s


def _params(n_axes):
    return pltpu.CompilerParams(dimension_semantics=("arbitrary",) * n_axes,
                                vmem_limit_bytes=VMEM_LIMIT)


def _norm_mm_kernel(x_ref, g_ref, w_ref, *rest, has_bias):
    if has_bias:
        b_ref, o_ref, xn_ref = rest
    else:
        o_ref, xn_ref = rest

    @pl.when(pl.program_id(1) == 0)
    def _():
        x = x_ref[...]
        ms = jnp.mean(x * x, axis=-1, keepdims=True)
        xn_ref[...] = (x * lax.rsqrt(ms + EPS) * g_ref[...]).astype(bf16)

    acc = jnp.dot(xn_ref[...], w_ref[...], preferred_element_type=f32)
    if has_bias:
        acc = acc + b_ref[...]
    o_ref[...] = acc


def _norm_mm(x, g, w, b, layer, *, tm, tn):
    m, d = x.shape
    n = w.shape[-1]
    in_specs = [
        pl.BlockSpec((tm, d), lambda i, j: (i, 0)),
        pl.BlockSpec((None, 1, d), lambda i, j: (layer, 0, 0)),
        pl.BlockSpec((None, d, tn), lambda i, j: (layer, 0, j)),
    ]
    args = [x, g, w]
    if b is not None:
        in_specs.append(pl.BlockSpec((None, 1, tn), lambda i, j: (layer, 0, j)))
        args.append(b)
    return pl.pallas_call(
        functools.partial(_norm_mm_kernel, has_bias=b is not None),
        grid=(m // tm, n // tn),
        in_specs=in_specs,
        out_specs=pl.BlockSpec((tm, tn), lambda i, j: (i, j)),
        out_shape=jax.ShapeDtypeStruct((m, n), f32),
        scratch_shapes=[pltpu.VMEM((tm, d), bf16)],
        compiler_params=_params(2),
        name="norm_mm",
    )(*args)


class _Seq:
    def __init__(self, batch, seq_pad, seq_real, rows, has_state):
        self.batch = batch
        self.seq_pad = seq_pad
        self.seq_real = seq_real
        self.rows = rows
        self.has_state = has_state
        self.short = seq_pad <= rows
        if self.short:
            assert rows % seq_pad == 0 and seq_pad == SUBLANES
            self.seqs_per_tile = rows // seq_pad
            self.tiles_per_seq = 1
            self.chunk = seq_pad
        else:
            assert seq_pad % rows == 0 and seq_pad == seq_real
            self.seqs_per_tile = 1
            self.tiles_per_seq = seq_pad // rows
            self.chunk = CHUNK
        self.n_tiles = batch * seq_pad // rows
        self.n_chunks = rows // self.chunk

    def state_spec(self, tail):
        nt = len(tail)
        tps = self.tiles_per_seq
        return pl.BlockSpec((self.seqs_per_tile,) + tuple(tail),
                            lambda i: (i // tps,) + (0,) * nt)

    def in_state_spec(self, tail, layer):
        nt = len(tail)
        return pl.BlockSpec((None, self.seqs_per_tile) + tuple(tail),
                            lambda i: (layer, i) + (0,) * nt)


def _conv_fill(seq, ucat_ref, u, state_ref, width):
    r = seq.rows
    i = pl.program_id(0)
    if seq.short:
        ucat_ref[pl.ds(SUBLANES, r), :] = u
        for s in range(seq.seqs_per_tile):
            lo = SUBLANES + s * seq.seq_pad - (width - 1)
            if seq.has_state:
                ucat_ref[pl.ds(lo, width - 1), :] = state_ref[s]
            else:
                ucat_ref[pl.ds(lo, width - 1), :] = jnp.zeros((width - 1, u.shape[-1]), f32)
    else:
        @pl.when(i % seq.tiles_per_seq == 0)
        def _():
            ucat_ref[pl.ds(0, SUBLANES), :] = jnp.zeros((SUBLANES, u.shape[-1]), f32)

        @pl.when(i % seq.tiles_per_seq != 0)
        def _():
            ucat_ref[pl.ds(0, SUBLANES), :] = ucat_ref[pl.ds(r, SUBLANES), :]

        ucat_ref[pl.ds(SUBLANES, r), :] = u


def _conv_apply(seq, ucat_ref, w_ref, b_ref, width):
    r = seq.rows
    y = b_ref[...]
    for j in range(width):
        y = y + ucat_ref[pl.ds(SUBLANES - (width - 1) + j, r), :] * w_ref[j:j + 1, :]
    return y


def _conv_state_out(seq, ucat_ref, out_ref, width):
    r = seq.rows
    i = pl.program_id(0)
    if seq.short:
        for s in range(seq.seqs_per_tile):
            lo = SUBLANES + s * seq.seq_pad + seq.seq_real - (width - 1)
            out_ref[s] = ucat_ref[pl.ds(lo, width - 1), :]
    else:
        @pl.when(i % seq.tiles_per_seq == seq.tiles_per_seq - 1)
        def _():
            out_ref[0] = ucat_ref[pl.ds(SUBLANES + r - (width - 1), width - 1), :]


def _gla_kernel(*refs, seq):
    if seq.has_state:
        (q_ref, k_ref, v_ref, g_ref, a_ref, wa_ref, ba_ref, gn_ref, s0_ref,
         o_ref, sout_ref, st_ref) = refs
    else:
        (q_ref, k_ref, v_ref, g_ref, a_ref, wa_ref, ba_ref, gn_ref,
         o_ref, sout_ref, st_ref) = refs
        s0_ref = None
    c = seq.chunk
    i = pl.program_id(0)
    mask = _tri(c)
    tri = mask.astype(f32)
    zpad = jnp.zeros((LANES - GLA_DK, GLA_DV), f32)

    def store_state(idx):
        for h in range(GLA_HEADS):
            sout_ref[idx, h] = st_ref[h].T[0:GLA_DK, :]

    if not seq.short:
        @pl.when(i % seq.tiles_per_seq == 0)
        def _():
            st_ref[...] = jnp.zeros_like(st_ref)

    def body(ci, carry):
        rows = pl.ds(pl.multiple_of(ci * c, c), c)
        if seq.short:
            for h in range(GLA_HEADS):
                if seq.has_state:
                    st_ref[h] = jnp.concatenate([s0_ref[ci, h], zpad], axis=0).T
                else:
                    st_ref[h] = jnp.zeros((GLA_DV, LANES), f32)
        q = q_ref[rows, :] * (GLA_DK ** -0.5)
        k = k_ref[rows, :]
        v = v_ref[rows, :]
        g = g_ref[rows, :]
        lg = _log_sigmoid(_dot(a_ref[rows, :], wa_ref[...]) + ba_ref[...]) / GLA_TAU
        if seq.seq_real < c:
            valid = lax.broadcasted_iota(jnp.int32, (c, 1), 0) < seq.seq_real
            lg = jnp.where(valid, lg, 0.0)
            k = jnp.where(valid, k, 0.0)
        bcum = _cumsum_rows(tri, lg)
        blast = bcum[c - 1:c, :]
        q_d = q * jnp.exp(bcum)
        k_d = k * jnp.exp(-bcum)
        k_e = k * jnp.exp(blast - bcum)
        dec = jnp.exp(blast)
        for h in range(GLA_HEADS):
            sl = slice(h * LANES, (h + 1) * LANES)
            a_mat = jnp.where(mask, _dot_nt(q_d[:, sl], k_d[:, sl]), 0.0)
            st = st_ref[h]
            o_h = _dot(a_mat, v[:, sl]) + _dot_nt(q_d[:, sl], st)
            st_ref[h] = dec[:, sl] * st + _dot_tn(v[:, sl], k_e[:, sl])
            o_ref[rows, sl] = _head_rmsnorm(o_h, gn_ref[:, sl]) * _silu(g[:, sl])
        if seq.short:
            store_state(ci)
        return carry

    lax.fori_loop(0, seq.n_chunks, body, 0)

    if not seq.short:
        @pl.when(i % seq.tiles_per_seq == seq.tiles_per_seq - 1)
        def _():
            store_state(0)


def _gla(proj, wa, ba, gn, s0, layer, seq):
    m = proj.shape[0]
    r = seq.rows

    def col(cb, w):
        return pl.BlockSpec((r, w), lambda i: (i, cb))

    in_specs = [col(COL_GQ // 512, 512), col(COL_GK // 512, 512), col(COL_GV // 512, 512),
                col(COL_GG // 512, 512), col(COL_GA // LANES, LANES),
                pl.BlockSpec((None, LANES, 512), lambda i: (layer, 0, 0)),
                pl.BlockSpec((None, 1, 512), lambda i: (layer, 0, 0)),
                pl.BlockSpec((None, 1, 512), lambda i: (layer, 0, 0))]
    args = [proj, proj, proj, proj, proj, wa, ba, gn]
    tail = (GLA_HEADS, GLA_DK, GLA_DV)
    if seq.has_state:
        in_specs.append(seq.in_state_spec(tail, layer))
        args.append(s0)
    return pl.pallas_call(
        functools.partial(_gla_kernel, seq=seq),
        grid=(seq.n_tiles,),
        in_specs=in_specs,
        out_specs=[pl.BlockSpec((r, 512), lambda i: (i, 0)), seq.state_spec(tail)],
        out_shape=[jax.ShapeDtypeStruct((m, 512), f32),
                   jax.ShapeDtypeStruct((seq.batch,) + tail, f32)],
        scratch_shapes=[pltpu.VMEM((GLA_HEADS, GLA_DV, LANES), f32)],
        compiler_params=_params(1),
        name="gla",
    )(*args)


def _mlstm_kernel(*refs, seq):
    n_in = 9
    (mu_ref, mo_ref, mif_ref, cw_ref, cb_ref, wq_ref, wk_ref, wv_ref, gn_ref) = refs[:n_in]
    if seq.has_state:
        c0_ref, n0_ref, m0_ref, conv0_ref = refs[n_in:n_in + 4]
        rest = refs[n_in + 4:]
    else:
        c0_ref = n0_ref = m0_ref = conv0_ref = None
        rest = refs[n_in:]
    (o_ref, cout_ref, nout_ref, mout_ref, convout_ref,
     ucat_ref, q_sc, k_sc, v_sc, c_st, n_st, m_st) = rest
    c = seq.chunk
    i = pl.program_id(0)
    mask = _tri(c)
    tri = mask.astype(f32)
    lane = lax.broadcasted_iota(jnp.int32, (c, LANES), 1)
    sel = (lax.broadcasted_iota(jnp.int32, (SUBLANES, LANES), 0)
           == lax.broadcasted_iota(jnp.int32, (SUBLANES, LANES), 1)).astype(f32)
    lane_h = lax.broadcasted_iota(jnp.int32, (1, M_HEADS), 1)

    mu = mu_ref[...]
    _conv_fill(seq, ucat_ref, mu, conv0_ref, CONV_W)
    mcv = _silu(_conv_apply(seq, ucat_ref, cw_ref, cb_ref, CONV_W))
    _conv_state_out(seq, ucat_ref, convout_ref, CONV_W)
    for h in range(M_HEADS):
        sl = slice(h * M_DH, (h + 1) * M_DH)
        q_sc[:, sl] = _dot(mcv[:, sl], wq_ref[h])
        k_sc[:, sl] = _dot(mcv[:, sl], wk_ref[h]) * (M_DH ** -0.5)
        v_sc[:, sl] = _dot(mu[:, sl], wv_ref[h])

    def store_state(idx):
        cout_ref[idx] = c_st[...]
        nout_ref[idx] = n_st[0:M_HEADS, :]
        m_row = jnp.zeros((1, M_HEADS), f32)
        for h in range(M_HEADS):
            m_row = jnp.where(lane_h == h, m_st[h:h + 1, 0:1], m_row)
        mout_ref[idx] = m_row

    if not seq.short:
        @pl.when(i % seq.tiles_per_seq == 0)
        def _():
            c_st[...] = jnp.zeros_like(c_st)
            n_st[...] = jnp.zeros_like(n_st)
            m_st[...] = jnp.zeros_like(m_st)

    def body(ci, carry):
        rows = pl.ds(pl.multiple_of(ci * c, c), c)
        if seq.short:
            if seq.has_state:
                c_st[...] = c0_ref[ci]
                n_st[0:M_HEADS, :] = n0_ref[ci]
                m0 = m0_ref[ci]
                for h in range(M_HEADS):
                    m_st[h:h + 1, :] = jnp.broadcast_to(m0[:, h:h + 1], (1, LANES))
            else:
                c_st[...] = jnp.zeros_like(c_st)
                n_st[...] = jnp.zeros_like(n_st)
                m_st[...] = jnp.zeros_like(m_st)
        gi = mif_ref[rows, :]
        lf = _log_sigmoid(gi)
        if seq.seq_real < c:
            valid_c = lax.broadcasted_iota(jnp.int32, (c, 1), 0) < seq.seq_real
            valid_r = lax.broadcasted_iota(jnp.int32, (1, c), 1) < seq.seq_real
            lf = jnp.where(valid_c, lf, 0.0)
        fcum = _cumsum_rows(tri, lf)
        x = jnp.where(lane < M_HEADS, gi, fcum)
        xt = lax.dot_general(sel, x, (((1,), (1,)), ((), ())), preferred_element_type=f32,
                             precision=lax.Precision.HIGHEST)
        q = q_sc[rows, :]
        k = k_sc[rows, :]
        v = v_sc[rows, :]
        mo = mo_ref[rows, :]
        for h in range(M_HEADS):
            sl = slice(h * M_DH, (h + 1) * M_DH)
            f_col = fcum[:, M_HEADS + h:M_HEADS + h + 1]
            i_col = gi[:, h:h + 1]
            f_row = xt[M_HEADS + h:M_HEADS + h + 1, :]
            i_row = xt[h:h + 1, :]
            if seq.seq_real < c:
                i_col = jnp.where(valid_c, i_col, -jnp.inf)
                i_row = jnp.where(valid_r, i_row, -jnp.inf)
            dm = jnp.where(mask, f_col - f_row + i_row, -jnp.inf)
            m_prev = m_st[h:h + 1, 0:1]
            b = f_col + m_prev
            mt = jnp.maximum(b, jnp.max(dm, axis=-1, keepdims=True))
            w_inter = jnp.exp(b - mt)
            qh, kh, vh = q[:, sl], k[:, sl], v[:, sl]
            p = jnp.exp(dm - mt) * _dot_nt(qh, kh)
            c_mat = c_st[h]
            n_vec = n_st[h:h + 1, :]
            num = w_inter * _dot_nt(qh, c_mat) + _dot(p, vh)
            den = (w_inter * jnp.sum(qh * n_vec, axis=-1, keepdims=True)
                   + jnp.sum(p, axis=-1, keepdims=True))
            hh = num / jnp.maximum(jnp.abs(den), jnp.exp(-mt))
            m_new = mt[c - 1:c, :]
            dec = jnp.exp(b[c - 1:c, :] - m_new)
            ws = jnp.exp(f_col[c - 1:c, :] - f_col + i_col - m_new)
            c_st[h] = dec * c_mat + _dot_tn(ws * vh, kh)
            n_st[h:h + 1, :] = dec * n_vec + jnp.sum(ws * kh, axis=0, keepdims=True)
            m_st[h:h + 1, :] = jnp.broadcast_to(m_new, (1, LANES))
            o_ref[rows, sl] = _head_rmsnorm(hh, gn_ref[:, sl]) * jax.nn.sigmoid(mo[:, sl])
        if seq.short:
            store_state(ci)
        return carry

    lax.fori_loop(0, seq.n_chunks, body, 0)

    if not seq.short:
        @pl.when(i % seq.tiles_per_seq == seq.tiles_per_seq - 1)
        def _():
            store_state(0)


def _mlstm(proj, cw, cb, wq, wk, wv, gn, states, layer, seq):
    m = proj.shape[0]
    r = seq.rows

    def col(cb_, w):
        return pl.BlockSpec((r, w), lambda i: (i, cb_))

    def lw(shape):
        n = len(shape)
        return pl.BlockSpec((None,) + shape, lambda i: (layer,) + (0,) * n)

    in_specs = [col(COL_MU // 512, 512), col(COL_MO // 512, 512), col(COL_MIF // LANES, LANES),
                lw((CONV_W, M_WIDTH)), lw((1, M_WIDTH)),
                lw((M_HEADS, M_DH, M_DH)), lw((M_HEADS, M_DH, M_DH)), lw((M_HEADS, M_DH, M_DH)),
                lw((1, M_WIDTH))]
    args = [proj, proj, proj, cw, cb, wq, wk, wv, gn]
    tails = [(M_HEADS, M_DH, M_DH), (M_HEADS, M_DH), (1, M_HEADS), (CONV_W - 1, M_WIDTH)]
    if seq.has_state:
        in_specs += [seq.in_state_spec(t, layer) for t in tails]
        args += list(states)
    return pl.pallas_call(
        functools.partial(_mlstm_kernel, seq=seq),
        grid=(seq.n_tiles,),
        in_specs=in_specs,
        out_specs=[pl.BlockSpec((r, M_WIDTH), lambda i: (i, 0))]
        + [seq.state_spec(t) for t in tails],
        out_shape=[jax.ShapeDtypeStruct((m, M_WIDTH), f32)]
        + [jax.ShapeDtypeStruct((seq.batch,) + t, f32) for t in tails],
        scratch_shapes=[pltpu.VMEM((r + SUBLANES, M_WIDTH), f32),
                        pltpu.VMEM((r, M_WIDTH), f32), pltpu.VMEM((r, M_WIDTH), f32),
                        pltpu.VMEM((r, M_WIDTH), f32),
                        pltpu.VMEM((M_HEADS, M_DH, M_DH), f32),
                        pltpu.VMEM((SUBLANES, LANES), f32), pltpu.VMEM((SUBLANES, LANES), f32)],
        compiler_params=_params(1),
        name="mlstm",
    )(*args)


def _rglru_kernel(*refs, seq):
    n_in = 9
    (rx_ref, ry_ref, cw_ref, cb_ref, wr_ref, br_ref, wi_ref, bi_ref, lam_ref) = refs[:n_in]
    if seq.has_state:
        h0_ref, conv0_ref = refs[n_in:n_in + 2]
        rest = refs[n_in + 2:]
    else:
        h0_ref = conv0_ref = None
        rest = refs[n_in:]
    o_ref, hout_ref, convout_ref, ucat_ref, a_sc, b_sc, h_st = rest
    r = seq.rows
    i = pl.program_id(0)

    _conv_fill(seq, ucat_ref, rx_ref[...], conv0_ref, CONV_W)
    xc = _conv_apply(seq, ucat_ref, cw_ref, cb_ref, CONV_W)
    _conv_state_out(seq, ucat_ref, convout_ref, CONV_W)
    r_g = jax.nn.sigmoid(_dot(xc, wr_ref[...]) + br_ref[...])
    i_g = jax.nn.sigmoid(_dot(xc, wi_ref[...]) + bi_ref[...])
    log_a = R_C * r_g * _log_sigmoid(lam_ref[...])
    a = jnp.exp(log_a)
    a_sc[...] = a
    b_sc[...] = jnp.sqrt(-jnp.tanh(log_a) * (a * a + 1.0)) * (i_g * xc)

    if not seq.short:
        @pl.when(i % seq.tiles_per_seq == 0)
        def _():
            h_st[...] = jnp.zeros_like(h_st)

    sub = lax.broadcasted_iota(jnp.int32, (SUBLANES, R_WIDTH), 0)

    def body(gi, h_prev):
        rows = pl.ds(pl.multiple_of(gi * SUBLANES, SUBLANES), SUBLANES)
        ag = a_sc[rows, :]
        bg = b_sc[rows, :]
        for d in (1, 2, 4):
            keep = sub >= d
            bg = jnp.where(keep, ag * pltpu.roll(bg, d, 0) + bg, bg)
            ag = jnp.where(keep, ag * pltpu.roll(ag, d, 0), ag)
        if seq.short:
            if seq.has_state:
                h_in = h0_ref[gi]
            else:
                h_in = jnp.zeros((1, R_WIDTH), f32)
        else:
            h_in = h_prev
        hg = bg + ag * h_in
        o_ref[rows, :] = hg * _gelu_tanh(ry_ref[rows, :])
        if seq.short:
            hout_ref[gi] = hg[seq.seq_real - 1:seq.seq_real, :]
            return h_prev
        return hg[SUBLANES - 1:SUBLANES, :]

    h_last = lax.fori_loop(0, r // SUBLANES, body, h_st[0:1, :])

    if not seq.short:
        h_st[0:1, :] = h_last

        @pl.when(i % seq.tiles_per_seq == seq.tiles_per_seq - 1)
        def _():
            hout_ref[0] = h_last


def _rglru(proj, cw, cb, wr, br, wi, bi, lam, states, layer, seq):
    m = proj.shape[0]
    r = seq.rows

    def col(cb_, w):
        return pl.BlockSpec((r, w), lambda i: (i, cb_))

    def lw(shape):
        n = len(shape)
        return pl.BlockSpec((None,) + shape, lambda i: (layer,) + (0,) * n)

    in_specs = [col(COL_RX // 512, 512), col(COL_RY // 512, 512),
                lw((CONV_W, R_WIDTH)), lw((1, R_WIDTH)),
                lw((R_WIDTH, R_WIDTH)), lw((1, R_WIDTH)),
                lw((R_WIDTH, R_WIDTH)), lw((1, R_WIDTH)), lw((1, R_WIDTH))]
    args = [proj, proj, cw, cb, wr, br, wi, bi, lam]
    tails = [(1, R_WIDTH), (CONV_W - 1, R_WIDTH)]
    if seq.has_state:
        in_specs += [seq.in_state_spec(t, layer) for t in tails]
        args += list(states)
    return pl.pallas_call(
        functools.partial(_rglru_kernel, seq=seq),
        grid=(seq.n_tiles,),
        in_specs=in_specs,
        out_specs=[pl.BlockSpec((r, R_WIDTH), lambda i: (i, 0))]
        + [seq.state_spec(t) for t in tails],
        out_shape=[jax.ShapeDtypeStruct((m, R_WIDTH), f32)]
        + [jax.ShapeDtypeStruct((seq.batch,) + t, f32) for t in tails],
        scratch_shapes=[pltpu.VMEM((r + SUBLANES, R_WIDTH), f32),
                        pltpu.VMEM((r, R_WIDTH), f32), pltpu.VMEM((r, R_WIDTH), f32),
                        pltpu.VMEM((SUBLANES, R_WIDTH), f32)],
        compiler_params=_params(1),
        name="rglru",
    )(*args)


def _merge_kernel(gates_ref, og_ref, om_ref, or_ref, x_ref, wb_ref, wo_ref, o_ref):
    mix = None
    for j, br_ref in enumerate((og_ref, om_ref, or_ref)):
        gate = jax.nn.sigmoid(gates_ref[:, j * D_MODEL:(j + 1) * D_MODEL])
        term = gate * _dot(br_ref[...], wb_ref[j * 512:(j + 1) * 512, :])
        mix = term if mix is None else mix + term
    o_ref[...] = x_ref[...] + _dot(mix, wo_ref[...])


def _merge(proj, o_g, o_m, o_r, x, wb, wo, layer, *, tm):
    m = x.shape[0]
    row = lambda w: pl.BlockSpec((tm, w), lambda i: (i, 0))
    return pl.pallas_call(
        _merge_kernel,
        grid=(m // tm,),
        in_specs=[row(3 * D_MODEL), row(512), row(512), row(512), row(D_MODEL),
                  pl.BlockSpec((None, 1536, D_MODEL), lambda i: (layer, 0, 0)),
                  pl.BlockSpec((None, D_MODEL, D_MODEL), lambda i: (layer, 0, 0))],
        out_specs=row(D_MODEL),
        out_shape=jax.ShapeDtypeStruct((m, D_MODEL), f32),
        compiler_params=_params(1),
        name="merge",
    )(proj, o_g, o_m, o_r, x, wb, wo)


def _ffn_down_kernel(*refs, seq, final_norm):
    n_in = 6
    u_ref, x_ref, cw_ref, cb_ref, wd_ref, gf_ref = refs[:n_in]
    if seq.has_state:
        conv0_ref = refs[n_in]
        rest = refs[n_in + 1:]
    else:
        conv0_ref = None
        rest = refs[n_in:]
    o_ref, convout_ref, ucat_ref = rest
    _conv_fill(seq, ucat_ref, u_ref[...], conv0_ref, FFN_CONV_W)
    uc = _conv_apply(seq, ucat_ref, cw_ref, cb_ref, FFN_CONV_W)
    _conv_state_out(seq, ucat_ref, convout_ref, FFN_CONV_W)
    act = _gelu_tanh(uc[:, :D_FF]) * uc[:, D_FF:]
    y = x_ref[...] + _dot(act, wd_ref[...])
    if final_norm:
        y = y * lax.rsqrt(jnp.mean(y * y, axis=-1, keepdims=True) + EPS) * gf_ref[...]
    o_ref[...] = y


def _ffn_down(u, x, cw, cb, wd, gf, conv0, layer, seq, final_norm):
    m = x.shape[0]
    r = seq.rows
    row = lambda w: pl.BlockSpec((r, w), lambda i: (i, 0))

    def lw(shape):
        n = len(shape)
        return pl.BlockSpec((None,) + shape, lambda i: (layer,) + (0,) * n)

    in_specs = [row(2 * D_FF), row(D_MODEL), lw((FFN_CONV_W, 2 * D_FF)), lw((1, 2 * D_FF)),
                lw((D_FF, D_MODEL)), pl.BlockSpec((1, D_MODEL), lambda i: (0, 0))]
    args = [u, x, cw, cb, wd, gf]
    tail = (FFN_CONV_W - 1, 2 * D_FF)
    if seq.has_state:
        in_specs.append(seq.in_state_spec(tail, layer))
        args.append(conv0)
    return pl.pallas_call(
        functools.partial(_ffn_down_kernel, seq=seq, final_norm=final_norm),
        grid=(seq.n_tiles,),
        in_specs=in_specs,
        out_specs=[row(D_MODEL), seq.state_spec(tail)],
        out_shape=[jax.ShapeDtypeStruct((m, D_MODEL), f32),
                   jax.ShapeDtypeStruct((seq.batch,) + tail, f32)],
        scratch_shapes=[pltpu.VMEM((r + SUBLANES, 2 * D_FF), f32)],
        compiler_params=_params(1),
        name="ffn_down",
    )(*args)


def _head_pad(a):
    lead = a.shape[:-1]
    a = a.reshape(lead + (GLA_HEADS, GLA_DK))
    a = jnp.pad(a, [(0, 0)] * len(lead) + [(0, 0), (0, LANES - GLA_DK)])
    return a.reshape(lead + (GLA_HEADS * LANES,))


def _lane_pad(a, width):
    return jnp.pad(a, [(0, 0)] * (a.ndim - 1) + [(0, width - a.shape[-1])])


def _permute_in_cols(a):
    sizes = (256, 256, 512, 512, GLA_RANK, 512, M_HEADS, M_HEADS, 512, 512, 512, 3 * D_MODEL)
    parts = []
    off = 0
    for s in sizes:
        parts.append(a[..., off:off + s])
        off += s
    gq, gk, gv, gg, ga, mu, mi, mf, mo, rx, ry, gates = parts
    out = jnp.concatenate(
        [gates, _head_pad(gq), _head_pad(gk), gv, gg, mu, mo, rx, ry,
         _lane_pad(ga, LANES), _lane_pad(jnp.concatenate([mi, mf], axis=-1), LANES)], axis=-1)
    return _lane_pad(out, D_IN_PAD)


def _block_diag(w):
    d, nb, bs, _ = w.shape
    eye = jnp.eye(nb, dtype=w.dtype)
    return jnp.einsum('lnde,nm->lndme', w, eye).reshape(d, nb * bs, nb * bs)


def _run_group(x, seq_mix, seq_ffn, states, wts, tm):
    per_layer = []
    for l in range(DEPTH):
        proj = _norm_mm(x, wts['norm_mix_g'], wts['w_in'], wts['b_in'], l, tm=tm, tn=1536)
        st = states
        o_g, s_new = _gla(proj, wts['w_gla_a2'], wts['b_gla_a2'], wts['gla_norm_g'],
                          st[0] if st else None, l, seq_mix)
        o_m, c_new, n_new, m_new, mconv_new = _mlstm(
            proj, wts['mlstm_conv_w'], wts['mlstm_conv_b'], wts['w_mlstm_q'], wts['w_mlstm_k'],
            wts['w_mlstm_v'], wts['mlstm_norm_g'], st[1:5] if st else None, l, seq_mix)
        o_r, h_new, rconv_new = _rglru(
            proj, wts['rglru_conv_w'], wts['rglru_conv_b'], wts['w_rglru_r'], wts['b_rglru_r'],
            wts['w_rglru_i'], wts['b_rglru_i'], wts['rglru_lambda'], st[5:7] if st else None,
            l, seq_mix)
        x = _merge(proj, o_g, o_m, o_r, x, wts['w_branch'], wts['w_out'], l, tm=min(tm, 512))
        u = _norm_mm(x, wts['norm_ffn_g'], wts['w_up'], None, l, tm=tm, tn=1408)
        x, fconv_new = _ffn_down(u, x, wts['ffn_conv_w'], wts['ffn_conv_b'], wts['w_down'],
                                 wts['norm_f_g'], st[7] if st else None, l, seq_ffn,
                                 final_norm=(l == DEPTH - 1))
        per_layer.append([s_new, c_new, n_new, m_new, mconv_new, h_new, rconv_new, fconv_new])
    new_states = [jnp.stack([st[j] for st in per_layer]) for j in range(8)]
    return x, new_states


def kernel(x_prompt, x_sample, state_gla_S, state_mlstm_C, state_mlstm_n, state_mlstm_m,
           state_mlstm_conv, state_rglru_h, state_rglru_conv, state_ffn_conv,
           norm_mix_g, w_in, b_in, w_gla_a2, b_gla_a2, gla_norm_g,
           mlstm_conv_w, mlstm_conv_b, w_mlstm_q, w_mlstm_k, w_mlstm_v, mlstm_norm_g,
           rglru_conv_w, rglru_conv_b, w_rglru_r, b_rglru_r, w_rglru_i, b_rglru_i, rglru_lambda,
           w_branch, w_out, norm_ffn_g, w_up, ffn_conv_w, ffn_conv_b, w_down, norm_f_g):
    row = lambda a: a[:, None, :]
    wts = {
        'norm_mix_g': row(norm_mix_g),
        'w_in': _permute_in_cols(w_in).astype(bf16),
        'b_in': row(_permute_in_cols(b_in)),
        'w_gla_a2': _head_pad(jnp.pad(w_gla_a2, ((0, 0), (0, LANES - GLA_RANK), (0, 0)))).astype(bf16),
        'b_gla_a2': row(_head_pad(b_gla_a2)),
        'gla_norm_g': row(gla_norm_g),
        'mlstm_conv_w': mlstm_conv_w, 'mlstm_conv_b': row(mlstm_conv_b),
        'w_mlstm_q': w_mlstm_q.astype(bf16), 'w_mlstm_k': w_mlstm_k.astype(bf16),
        'w_mlstm_v': w_mlstm_v.astype(bf16), 'mlstm_norm_g': row(mlstm_norm_g),
        'rglru_conv_w': rglru_conv_w, 'rglru_conv_b': row(rglru_conv_b),
        'w_rglru_r': _block_diag(w_rglru_r).astype(bf16), 'b_rglru_r': row(b_rglru_r),
        'w_rglru_i': _block_diag(w_rglru_i).astype(bf16), 'b_rglru_i': row(b_rglru_i),
        'rglru_lambda': row(rglru_lambda),
        'w_branch': w_branch.astype(bf16), 'w_out': w_out.astype(bf16),
        'norm_ffn_g': row(norm_ffn_g), 'w_up': w_up.astype(bf16),
        'ffn_conv_w': ffn_conv_w, 'ffn_conv_b': row(ffn_conv_b),
        'w_down': w_down.astype(bf16), 'norm_f_g': norm_f_g[None, :],
    }

    bp, lp, _ = x_prompt.shape
    seq_p_mix = _Seq(bp, lp, lp, 512, has_state=False)
    seq_p_ffn = _Seq(bp, lp, lp, 256, has_state=False)
    y_p, p_states = _run_group(x_prompt.reshape(bp * lp, D_MODEL), seq_p_mix, seq_p_ffn,
                               None, wts, tm=1024)
    y_prompt = y_p.reshape(bp, lp, D_MODEL)

    bs, ls, _ = x_sample.shape
    xs = jnp.pad(x_sample, ((0, 0), (0, SUBLANES - ls), (0, 0))).reshape(bs * SUBLANES, D_MODEL)
    seq_s = _Seq(bs, SUBLANES, ls, 128, has_state=True)
    s_in = [state_gla_S, state_mlstm_C, state_mlstm_n, state_mlstm_m[:, :, None, :],
            state_mlstm_conv, state_rglru_h[:, :, None, :], state_rglru_conv, state_ffn_conv]
    y_s, s_states = _run_group(xs, seq_s, seq_s, s_in, wts, tm=1024)
    y_sample = y_s.reshape(bs, SUBLANES, D_MODEL)[:, :ls]

    def unpack(st):
        s, c, n, m, mconv, h, rconv, fconv = st
        return (s, c, n, m[:, :, 0, :], mconv, h[:, :, 0, :], rconv, fconv)

    return (y_prompt, y_sample) + unpack(p_states) + unpack(s_states)
```

```python
import functools

import jax
import jax.numpy as jnp
from jax import lax
from jax.experimental import pallas as pl
from jax.experimental.pallas import tpu as pltpu

f32 = jnp.float32
bf16 = jnp.bfloat16

D_MODEL = 1024
DEPTH = 4
GLA_HEADS = 4
GLA_DK = 64
GLA_DV = 128
GLA_RANK = 16
GLA_TAU = 16.0
M_HEADS = 4
M_DH = 128
M_WIDTH = 512
CONV_W = 4
R_WIDTH = 512
R_BLOCKS = 8
R_BS = 64
R_C = 8.0
D_FF = 2816
FFN_CONV_W = 3
EPS = 1e-6

LANES = 128
SUBLANES = 8
GLA_CHUNK = 64
MLSTM_CHUNK = 256
SHORT_UNROLL = 4
CONV_ROWS = 64
CONV_COLS = 256
VMEM_LIMIT = 52 * 1024 * 1024

COL_GATES = 0
COL_GQ = 3072
COL_GK = 3584
COL_GV = 4096
COL_GG = 4608
COL_MU = 5120
COL_MO = 5632
COL_RX = 6144
COL_RY = 6656
COL_GA = 7168
COL_MIF = 7296
D_IN_PAD = 7680


def _log_sigmoid(x):
    return jnp.minimum(x, 0.0) - jnp.log1p(jnp.exp(-jnp.abs(x)))


def _silu(x):
    return x * jax.nn.sigmoid(x)


def _gelu_tanh(x):
    hx = 0.5 * x
    return hx + hx * jnp.tanh(x * (0.7978845608028654 + (0.7978845608028654 * 0.044715) * (x * x)))


def _head_rmsnorm(o, g):
    return o * lax.rsqrt(jnp.mean(o * o, axis=-1, keepdims=True) + EPS) * g


def _dot(a, b):
    return jnp.dot(a.astype(bf16), b.astype(bf16), preferred_element_type=f32)


def _dot_nt(a, b):
    return lax.dot_general(a.astype(bf16), b.astype(bf16), (((1,), (1,)), ((), ())),
                           preferred_element_type=f32)


def _dot_tn(a, b):
    return lax.dot_general(a.astype(bf16), b.astype(bf16), (((0,), (0,)), ((), ())),
                           preferred_element_type=f32)


def _cumsum_rows(tri, x):
    return jnp.dot(tri, x, preferred_element_type=f32, precision=lax.Precision.HIGHEST)


def _tri(c):
    r = lax.broadcasted_iota(jnp.int32, (c, c), 0)
    s = lax.broadcasted_iota(jnp.int32, (c, c), 1)
    return r >= s


def _params(n_axes):
    return pltpu.CompilerParams(dimension_semantics=("arbitrary",) * n_axes,
                                vmem_limit_bytes=VMEM_LIMIT)


def _in_proj_kernel(x_ref, g_ref, w_ref, b_ref, o_ref, xn_ref):
    @pl.when(pl.program_id(1) == 0)
    def _():
        x = x_ref[...]
        ms = jnp.mean(x * x, axis=-1, keepdims=True)
        xn_ref[...] = (x * lax.rsqrt(ms + EPS) * g_ref[...]).astype(bf16)

    o_ref[...] = jnp.dot(xn_ref[...], w_ref[...], preferred_element_type=f32) + b_ref[...]


def _in_proj(x, g, w, b, layer, *, tm, tn):
    m, d = x.shape
    n = w.shape[-1]
    return pl.pallas_call(
        _in_proj_kernel,
        grid=(m // tm, n // tn),
        in_specs=[pl.BlockSpec((tm, d), lambda i, j: (i, 0)),
                  pl.BlockSpec((None, 1, d), lambda i, j: (layer, 0, 0)),
                  pl.BlockSpec((None, d, tn), lambda i, j: (layer, 0, j)),
                  pl.BlockSpec((None, 1, tn), lambda i, j: (layer, 0, j))],
        out_specs=pl.BlockSpec((tm, tn), lambda i, j: (i, j)),
        out_shape=jax.ShapeDtypeStruct((m, n), f32),
        scratch_shapes=[pltpu.VMEM((tm, d), bf16)],
        compiler_params=_params(2),
        name="in_proj",
    )(x, g, w, b)


class _Seq:
    def __init__(self, batch, seq_pad, seq_real, rows, has_state):
        self.batch = batch
        self.seq_pad = seq_pad
        self.seq_real = seq_real
        self.rows = rows
        self.has_state = has_state
        self.short = seq_pad <= rows
        if self.short:
            assert rows % seq_pad == 0 and seq_pad == SUBLANES
            self.seqs_per_tile = rows // seq_pad
            self.tiles_per_seq = 1
            self.gla_chunk = self.mlstm_chunk = seq_pad
        else:
            assert seq_pad % rows == 0 and seq_pad == seq_real
            self.seqs_per_tile = 1
            self.tiles_per_seq = seq_pad // rows
            self.gla_chunk = GLA_CHUNK
            self.mlstm_chunk = min(MLSTM_CHUNK, rows)
        self.n_tiles = batch * seq_pad // rows

    def state_spec(self, tail):
        nt = len(tail)
        tps = self.tiles_per_seq
        return pl.BlockSpec((self.seqs_per_tile,) + tuple(tail),
                            lambda i: (i // tps,) + (0,) * nt)

    def in_state_spec(self, tail, layer):
        nt = len(tail)
        return pl.BlockSpec((None, self.seqs_per_tile) + tuple(tail),
                            lambda i: (layer, i) + (0,) * nt)


def _conv_carry(seq, ucat_ref):
    if seq.short:
        return
    r = seq.rows
    i = pl.program_id(0)
    width = ucat_ref.shape[-1]

    @pl.when(i % seq.tiles_per_seq == 0)
    def _():
        ucat_ref[pl.ds(0, SUBLANES), :] = jnp.zeros((SUBLANES, width), f32)

    @pl.when(i % seq.tiles_per_seq != 0)
    def _():
        ucat_ref[pl.ds(0, SUBLANES), :] = ucat_ref[pl.ds(r, SUBLANES), :]


def _conv_store(seq, ucat_ref, u, state_ref, width, cols):
    ucat_ref[pl.ds(SUBLANES, seq.rows), cols] = u
    if seq.short:
        for s in range(seq.seqs_per_tile):
            lo = SUBLANES + s * seq.seq_pad - (width - 1)
            if seq.has_state:
                ucat_ref[pl.ds(lo, width - 1), cols] = state_ref[s, :, cols]
            else:
                ucat_ref[pl.ds(lo, width - 1), cols] = jnp.zeros((width - 1, u.shape[-1]), f32)


def _conv_piece(ucat_ref, w_ref, b_ref, r0, rows, cols, width):
    u = ucat_ref[pl.ds(r0, rows + SUBLANES), cols]
    s = u * w_ref[0:1, cols]
    for j in range(1, width - 1):
        s = u * w_ref[j:j + 1, cols] + pltpu.roll(s, 1, 0)
    y = (b_ref[:, cols] + u * w_ref[width - 1:width, cols]) + pltpu.roll(s, 1, 0)
    return y[SUBLANES:, :]


def _conv_state_out(seq, ucat_ref, out_ref, width):
    r = seq.rows
    i = pl.program_id(0)
    if seq.short:
        for s in range(seq.seqs_per_tile):
            lo = SUBLANES + s * seq.seq_pad + seq.seq_real - (width - 1)
            out_ref[s] = ucat_ref[pl.ds(lo, width - 1), :]
    else:
        @pl.when(i % seq.tiles_per_seq == seq.tiles_per_seq - 1)
        def _():
            out_ref[0] = ucat_ref[pl.ds(SUBLANES + r - (width - 1), width - 1), :]


def _gla_kernel(*refs, seq):
    if seq.has_state:
        (q_ref, k_ref, v_ref, g_ref, a_ref, wa_ref, ba_ref, gn_ref, s0_ref,
         o_ref, sout_ref, st_ref) = refs
    else:
        (q_ref, k_ref, v_ref, g_ref, a_ref, wa_ref, ba_ref, gn_ref,
         o_ref, sout_ref, st_ref) = refs
        s0_ref = None
    c = seq.gla_chunk
    i = pl.program_id(0)
    mask = _tri(c)
    tri = mask.astype(f32)
    zpad = jnp.zeros((LANES - GLA_DK, GLA_DV), f32)

    def chunk(rows, st):
        q = q_ref[rows, :] * (GLA_DK ** -0.5)
        k = k_ref[rows, :]
        v = v_ref[rows, :]
        g = g_ref[rows, :]
        lg = _log_sigmoid(_dot(a_ref[rows, :], wa_ref[...]) + ba_ref[...]) / GLA_TAU
        if seq.seq_real < c:
            valid = lax.broadcasted_iota(jnp.int32, (c, 1), 0) < seq.seq_real
            lg = jnp.where(valid, lg, 0.0)
            k = jnp.where(valid, k, 0.0)
        bcum = _cumsum_rows(tri, lg)
        blast = bcum[c - 1:c, :]
        q_d = q * jnp.exp(bcum)
        k_d = k * jnp.exp(-bcum)
        k_e = k * jnp.exp(blast - bcum)
        dec = jnp.exp(blast)
        new_st = []
        for h in range(GLA_HEADS):
            sl = slice(h * LANES, (h + 1) * LANES)
            a_mat = jnp.where(mask, _dot_nt(q_d[:, sl], k_d[:, sl]), 0.0)
            o_h = _dot(a_mat, v[:, sl]) + _dot_nt(q_d[:, sl], st[h])
            new_st.append(dec[:, sl] * st[h] + _dot_tn(v[:, sl], k_e[:, sl]))
            o_ref[rows, sl] = _head_rmsnorm(o_h, gn_ref[:, sl]) * _silu(g[:, sl])
        return new_st

    if seq.short:
        def body(ci, carry):
            rows = pl.ds(pl.multiple_of(ci * c, c), c)
            if seq.has_state:
                st = [jnp.concatenate([s0_ref[ci, h], zpad], axis=0).T for h in range(GLA_HEADS)]
            else:
                st = [jnp.zeros((GLA_DV, LANES), f32)] * GLA_HEADS
            st = chunk(rows, st)
            for h in range(GLA_HEADS):
                sout_ref[ci, h] = st[h].T[0:GLA_DK, :]
            return carry

        lax.fori_loop(0, seq.rows // c, body, 0, unroll=SHORT_UNROLL)
    else:
        @pl.when(i % seq.tiles_per_seq == 0)
        def _():
            st_ref[...] = jnp.zeros_like(st_ref)

        st = [st_ref[h] for h in range(GLA_HEADS)]
        for ci in range(seq.rows // c):
            st = chunk(pl.ds(ci * c, c), st)
        for h in range(GLA_HEADS):
            st_ref[h] = st[h]

        @pl.when(i % seq.tiles_per_seq == seq.tiles_per_seq - 1)
        def _():
            for h in range(GLA_HEADS):
                sout_ref[0, h] = st[h].T[0:GLA_DK, :]


def _gla(proj, wa, ba, gn, s0, layer, seq):
    m = proj.shape[0]
    r = seq.rows

    def col(cb, w):
        return pl.BlockSpec((r, w), lambda i: (i, cb))

    in_specs = [col(COL_GQ // 512, 512), col(COL_GK // 512, 512), col(COL_GV // 512, 512),
                col(COL_GG // 512, 512), col(COL_GA // LANES, LANES),
                pl.BlockSpec((None, LANES, 512), lambda i: (layer, 0, 0)),
                pl.BlockSpec((None, 1, 512), lambda i: (layer, 0, 0)),
                pl.BlockSpec((None, 1, 512), lambda i: (layer, 0, 0))]
    args = [proj, proj, proj, proj, proj, wa, ba, gn]
    tail = (GLA_HEADS, GLA_DK, GLA_DV)
    if seq.has_state:
        in_specs.append(seq.in_state_spec(tail, layer))
        args.append(s0)
    return pl.pallas_call(
        functools.partial(_gla_kernel, seq=seq),
        grid=(seq.n_tiles,),
        in_specs=in_specs,
        out_specs=[pl.BlockSpec((r, 512), lambda i: (i, 0)), seq.state_spec(tail)],
        out_shape=[jax.ShapeDtypeStruct((m, 512), f32),
                   jax.ShapeDtypeStruct((seq.batch,) + tail, f32)],
        scratch_shapes=[pltpu.VMEM((GLA_HEADS, GLA_DV, LANES), f32)],
        compiler_params=_params(1),
        name="gla",
    )(*args)


def _mlstm_kernel(*refs, seq):
    n_in = 9
    (mu_ref, mo_ref, mif_ref, cw_ref, cb_ref, wq_ref, wk_ref, wv_ref, gn_ref) = refs[:n_in]
    if seq.has_state:
        c0_ref, n0_ref, m0_ref, conv0_ref = refs[n_in:n_in + 4]
        rest = refs[n_in + 4:]
    else:
        c0_ref = n0_ref = m0_ref = conv0_ref = None
        rest = refs[n_in:]
    (o_ref, cout_ref, nout_ref, mout_ref, convout_ref,
     ucat_ref, mcv_sc, q_sc, k_sc, v_sc, c_st, n_st, m_st) = rest
    c = seq.mlstm_chunk
    i = pl.program_id(0)
    mask = _tri(c)
    tri = mask.astype(f32)
    lane = lax.broadcasted_iota(jnp.int32, (c, LANES), 1)
    sel = (lax.broadcasted_iota(jnp.int32, (SUBLANES, LANES), 0)
           == lax.broadcasted_iota(jnp.int32, (SUBLANES, LANES), 1)).astype(f32)
    lane_h = lax.broadcasted_iota(jnp.int32, (1, M_HEADS), 1)

    mu = mu_ref[...]
    _conv_carry(seq, ucat_ref)
    _conv_store(seq, ucat_ref, mu, conv0_ref, CONV_W, slice(0, M_WIDTH))
    _conv_state_out(seq, ucat_ref, convout_ref, CONV_W)
    for r0 in range(0, seq.rows, CONV_ROWS):
        for c0 in range(0, M_WIDTH, CONV_COLS):
            cols = slice(c0, c0 + CONV_COLS)
            mcv_sc[pl.ds(r0, CONV_ROWS), cols] = _silu(
                _conv_piece(ucat_ref, cw_ref, cb_ref, r0, CONV_ROWS, cols, CONV_W)).astype(bf16)
    for h in range(M_HEADS):
        sl = slice(h * M_DH, (h + 1) * M_DH)
        q_sc[:, sl] = _dot(mcv_sc[:, sl], wq_ref[h])
        k_sc[:, sl] = _dot(mcv_sc[:, sl], wk_ref[h]) * (M_DH ** -0.5)
        v_sc[:, sl] = _dot(mu[:, sl], wv_ref[h])

    def chunk(rows, c_mats, n_vecs, m_prevs):
        gi = mif_ref[rows, :]
        lf = _log_sigmoid(gi)
        if seq.seq_real < c:
            valid_c = lax.broadcasted_iota(jnp.int32, (c, 1), 0) < seq.seq_real
            valid_r = lax.broadcasted_iota(jnp.int32, (1, c), 1) < seq.seq_real
            lf = jnp.where(valid_c, lf, 0.0)
        fcum = _cumsum_rows(tri, lf)
        x = jnp.where(lane < M_HEADS, gi, fcum)
        xt = lax.dot_general(sel, x, (((1,), (1,)), ((), ())), preferred_element_type=f32,
                             precision=lax.Precision.HIGHEST)
        q = q_sc[rows, :]
        k = k_sc[rows, :]
        v = v_sc[rows, :]
        mo = mo_ref[rows, :]
        new_c, new_n, new_m = [], [], []
        for h in range(M_HEADS):
            sl = slice(h * M_DH, (h + 1) * M_DH)
            f_col = fcum[:, M_HEADS + h:M_HEADS + h + 1]
            i_col = gi[:, h:h + 1]
            f_row = xt[M_HEADS + h:M_HEADS + h + 1, :]
            i_row = xt[h:h + 1, :]
            if seq.seq_real < c:
                i_col = jnp.where(valid_c, i_col, -jnp.inf)
                i_row = jnp.where(valid_r, i_row, -jnp.inf)
            dm = jnp.where(mask, f_col - f_row + i_row, -jnp.inf)
            b = f_col + m_prevs[h]
            mt = jnp.maximum(b, jnp.max(dm, axis=-1, keepdims=True))
            w_inter = jnp.exp(b - mt)
            qh, kh, vh = q[:, sl], k[:, sl], v[:, sl]
            p = jnp.exp(dm - mt) * _dot_nt(qh, kh)
            num = w_inter * _dot_nt(qh, c_mats[h]) + _dot(p, vh)
            den = (w_inter * jnp.sum(qh * n_vecs[h], axis=-1, keepdims=True)
                   + jnp.sum(p, axis=-1, keepdims=True))
            hh = num / jnp.maximum(jnp.abs(den), jnp.exp(-mt))
            m_new = mt[c - 1:c, :]
            dec = jnp.exp(b[c - 1:c, :] - m_new)
            ws = jnp.exp(f_col[c - 1:c, :] - f_col + i_col - m_new)
            new_c.append(dec * c_mats[h] + _dot_tn(ws * vh, kh))
            new_n.append(dec * n_vecs[h] + jnp.sum(ws * kh, axis=0, keepdims=True))
            new_m.append(m_new)
            o_ref[rows, sl] = _head_rmsnorm(hh, gn_ref[:, sl]) * jax.nn.sigmoid(mo[:, sl])
        return new_c, new_n, new_m

    def store_state(idx, c_mats, n_vecs, m_prevs):
        m_row = jnp.zeros((1, M_HEADS), f32)
        for h in range(M_HEADS):
            cout_ref[idx, h] = c_mats[h]
            nout_ref[idx, pl.ds(h, 1), :] = n_vecs[h]
            m_row = jnp.where(lane_h == h, m_prevs[h], m_row)
        mout_ref[idx] = m_row

    heads = range(M_HEADS)
    if seq.short:
        def body(ci, carry):
            rows = pl.ds(pl.multiple_of(ci * c, c), c)
            if seq.has_state:
                n0 = n0_ref[ci]
                m0 = m0_ref[ci]
                state = ([c0_ref[ci, h] for h in heads], [n0[h:h + 1, :] for h in heads],
                         [m0[:, h:h + 1] for h in heads])
            else:
                state = ([jnp.zeros((M_DH, M_DH), f32)] * M_HEADS,
                         [jnp.zeros((1, M_DH), f32)] * M_HEADS,
                         [jnp.zeros((1, 1), f32)] * M_HEADS)
            store_state(ci, *chunk(rows, *state))
            return carry

        lax.fori_loop(0, seq.rows // c, body, 0, unroll=SHORT_UNROLL)
    else:
        @pl.when(i % seq.tiles_per_seq == 0)
        def _():
            c_st[...] = jnp.zeros_like(c_st)
            n_st[...] = jnp.zeros_like(n_st)
            m_st[...] = jnp.zeros_like(m_st)

        state = ([c_st[h] for h in heads], [n_st[h:h + 1, :] for h in heads],
                 [m_st[h:h + 1, 0:1] for h in heads])
        for ci in range(seq.rows // c):
            state = chunk(pl.ds(ci * c, c), *state)
        for h in heads:
            c_st[h] = state[0][h]
            n_st[h:h + 1, :] = state[1][h]
            m_st[h:h + 1, :] = jnp.broadcast_to(state[2][h], (1, LANES))

        @pl.when(i % seq.tiles_per_seq == seq.tiles_per_seq - 1)
        def _():
            store_state(0, *state)


def _mlstm(proj, cw, cb, wq, wk, wv, gn, states, layer, seq):
    m = proj.shape[0]
    r = seq.rows

    def col(cb_, w):
        return pl.BlockSpec((r, w), lambda i: (i, cb_))

    def lw(shape):
        n = len(shape)
        return pl.BlockSpec((None,) + shape, lambda i: (layer,) + (0,) * n)

    in_specs = [col(COL_MU // 512, 512), col(COL_MO // 512, 512), col(COL_MIF // LANES, LANES),
                lw((CONV_W, M_WIDTH)), lw((1, M_WIDTH)),
                lw((M_HEADS, M_DH, M_DH)), lw((M_HEADS, M_DH, M_DH)), lw((M_HEADS, M_DH, M_DH)),
                lw((1, M_WIDTH))]
    args = [proj, proj, proj, cw, cb, wq, wk, wv, gn]
    tails = [(M_HEADS, M_DH, M_DH), (M_HEADS, M_DH), (1, M_HEADS), (CONV_W - 1, M_WIDTH)]
    if seq.has_state:
        in_specs += [seq.in_state_spec(t, layer) for t in tails]
        args += list(states)
    return pl.pallas_call(
        functools.partial(_mlstm_kernel, seq=seq),
        grid=(seq.n_tiles,),
        in_specs=in_specs,
        out_specs=[pl.BlockSpec((r, M_WIDTH), lambda i: (i, 0))]
        + [seq.state_spec(t) for t in tails],
        out_shape=[jax.ShapeDtypeStruct((m, M_WIDTH), f32)]
        + [jax.ShapeDtypeStruct((seq.batch,) + t, f32) for t in tails],
        scratch_shapes=[pltpu.VMEM((r + SUBLANES, M_WIDTH), f32), pltpu.VMEM((r, M_WIDTH), bf16),
                        pltpu.VMEM((r, M_WIDTH), f32), pltpu.VMEM((r, M_WIDTH), f32),
                        pltpu.VMEM((r, M_WIDTH), f32),
                        pltpu.VMEM((M_HEADS, M_DH, M_DH), f32),
                        pltpu.VMEM((SUBLANES, LANES), f32), pltpu.VMEM((SUBLANES, LANES), f32)],
        compiler_params=_params(1),
        name="mlstm",
    )(*args)


def _rglru_kernel(*refs, seq):
    n_in = 9
    (rx_ref, ry_ref, cw_ref, cb_ref, wr_ref, br_ref, wi_ref, bi_ref, lam_ref) = refs[:n_in]
    if seq.has_state:
        h0_ref, conv0_ref = refs[n_in:n_in + 2]
        rest = refs[n_in + 2:]
    else:
        h0_ref = conv0_ref = None
        rest = refs[n_in:]
    o_ref, hout_ref, convout_ref, ucat_ref, xc_sc, a_sc, b_sc, h_st = rest
    r = seq.rows
    i = pl.program_id(0)

    _conv_carry(seq, ucat_ref)
    _conv_store(seq, ucat_ref, rx_ref[...], conv0_ref, CONV_W, slice(0, R_WIDTH))
    _conv_state_out(seq, ucat_ref, convout_ref, CONV_W)
    pieces = [(r0, slice(c0, c0 + CONV_COLS))
              for r0 in range(0, r, CONV_ROWS) for c0 in range(0, R_WIDTH, CONV_COLS)]
    for r0, cols in pieces:
        xc_sc[pl.ds(r0, CONV_ROWS), cols] = _conv_piece(ucat_ref, cw_ref, cb_ref, r0, CONV_ROWS,
                                                        cols, CONV_W)
    xc_bf = xc_sc[...].astype(bf16)
    a_sc[...] = jnp.dot(xc_bf, wr_ref[...], preferred_element_type=f32) + br_ref[...]
    b_sc[...] = jnp.dot(xc_bf, wi_ref[...], preferred_element_type=f32) + bi_ref[...]
    log_lam = _log_sigmoid(lam_ref[...])
    for r0, cols in pieces:
        rows = pl.ds(r0, CONV_ROWS)
        log_a = R_C * jax.nn.sigmoid(a_sc[rows, cols]) * log_lam[:, cols]
        a = jnp.exp(log_a)
        gated_x = jax.nn.sigmoid(b_sc[rows, cols]) * xc_sc[rows, cols]
        a_sc[rows, cols] = a
        b_sc[rows, cols] = jnp.sqrt(-jnp.tanh(log_a) * (a * a + 1.0)) * gated_x

    if not seq.short:
        @pl.when(i % seq.tiles_per_seq == 0)
        def _():
            h_st[...] = jnp.zeros_like(h_st)

    sub = lax.broadcasted_iota(jnp.int32, (SUBLANES, R_WIDTH), 0)

    def body(gi, h_prev):
        rows = pl.ds(pl.multiple_of(gi * SUBLANES, SUBLANES), SUBLANES)
        ag = a_sc[rows, :]
        bg = b_sc[rows, :]
        for d in (1, 2, 4):
            keep = sub >= d
            bg = jnp.where(keep, ag * pltpu.roll(bg, d, 0) + bg, bg)
            ag = jnp.where(keep, ag * pltpu.roll(ag, d, 0), ag)
        if seq.short:
            if seq.has_state:
                h_in = h0_ref[gi]
            else:
                h_in = jnp.zeros((1, R_WIDTH), f32)
        else:
            h_in = h_prev
        hg = bg + ag * h_in
        o_ref[rows, :] = hg * _gelu_tanh(ry_ref[rows, :])
        if seq.short:
            hout_ref[gi] = hg[seq.seq_real - 1:seq.seq_real, :]
            return h_prev
        return hg[SUBLANES - 1:SUBLANES, :]

    h_last = lax.fori_loop(0, r // SUBLANES, body, h_st[0:1, :])

    if not seq.short:
        h_st[0:1, :] = h_last

        @pl.when(i % seq.tiles_per_seq == seq.tiles_per_seq - 1)
        def _():
            hout_ref[0] = h_last


def _rglru(proj, cw, cb, wr, br, wi, bi, lam, states, layer, seq):
    m = proj.shape[0]
    r = seq.rows

    def col(cb_, w):
        return pl.BlockSpec((r, w), lambda i: (i, cb_))

    def lw(shape):
        n = len(shape)
        return pl.BlockSpec((None,) + shape, lambda i: (layer,) + (0,) * n)

    in_specs = [col(COL_RX // 512, 512), col(COL_RY // 512, 512),
                lw((CONV_W, R_WIDTH)), lw((1, R_WIDTH)),
                lw((R_WIDTH, R_WIDTH)), lw((1, R_WIDTH)),
                lw((R_WIDTH, R_WIDTH)), lw((1, R_WIDTH)), lw((1, R_WIDTH))]
    args = [proj, proj, cw, cb, wr, br, wi, bi, lam]
    tails = [(1, R_WIDTH), (CONV_W - 1, R_WIDTH)]
    if seq.has_state:
        in_specs += [seq.in_state_spec(t, layer) for t in tails]
        args += list(states)
    return pl.pallas_call(
        functools.partial(_rglru_kernel, seq=seq),
        grid=(seq.n_tiles,),
        in_specs=in_specs,
        out_specs=[pl.BlockSpec((r, R_WIDTH), lambda i: (i, 0))]
        + [seq.state_spec(t) for t in tails],
        out_shape=[jax.ShapeDtypeStruct((m, R_WIDTH), f32)]
        + [jax.ShapeDtypeStruct((seq.batch,) + t, f32) for t in tails],
        scratch_shapes=[pltpu.VMEM((r + SUBLANES, R_WIDTH), f32), pltpu.VMEM((r, R_WIDTH), f32),
                        pltpu.VMEM((r, R_WIDTH), f32), pltpu.VMEM((r, R_WIDTH), f32),
                        pltpu.VMEM((SUBLANES, R_WIDTH), f32)],
        compiler_params=_params(1),
        name="rglru",
    )(*args)


def _merge_kernel(gates_ref, og_ref, om_ref, or_ref, x_ref, wb_ref, wo_ref, o_ref):
    mix = None
    for j, br_ref in enumerate((og_ref, om_ref, or_ref)):
        gate = jax.nn.sigmoid(gates_ref[:, j * D_MODEL:(j + 1) * D_MODEL])
        term = gate * _dot(br_ref[...], wb_ref[j * 512:(j + 1) * 512, :])
        mix = term if mix is None else mix + term
    o_ref[...] = x_ref[...] + _dot(mix, wo_ref[...])


def _merge(proj, o_g, o_m, o_r, x, wb, wo, layer, *, tm):
    m = x.shape[0]
    row = lambda w: pl.BlockSpec((tm, w), lambda i: (i, 0))
    return pl.pallas_call(
        _merge_kernel,
        grid=(m // tm,),
        in_specs=[row(3 * D_MODEL), row(512), row(512), row(512), row(D_MODEL),
                  pl.BlockSpec((None, 1536, D_MODEL), lambda i: (layer, 0, 0)),
                  pl.BlockSpec((None, D_MODEL, D_MODEL), lambda i: (layer, 0, 0))],
        out_specs=row(D_MODEL),
        out_shape=jax.ShapeDtypeStruct((m, D_MODEL), f32),
        compiler_params=_params(1),
        name="merge",
    )(proj, o_g, o_m, o_r, x, wb, wo)


def _ffn_kernel(*refs, seq, final_norm):
    n_in = 7
    x_ref, g_ref, wu_ref, cw_ref, cb_ref, wd_ref, gf_ref = refs[:n_in]
    if seq.has_state:
        conv0_ref = refs[n_in]
        rest = refs[n_in + 1:]
    else:
        conv0_ref = None
        rest = refs[n_in:]
    o_ref, convout_ref, ucat_ref, act_sc = rest
    x = x_ref[...]
    xn = (x * lax.rsqrt(jnp.mean(x * x, axis=-1, keepdims=True) + EPS) * g_ref[...]).astype(bf16)
    _conv_carry(seq, ucat_ref)
    for c0 in range(0, D_FF, CONV_COLS):
        halves = [slice(base + c0, base + c0 + CONV_COLS) for base in (0, D_FF)]
        for cols in halves:
            u = jnp.dot(xn, wu_ref[:, cols], preferred_element_type=f32)
            _conv_store(seq, ucat_ref, u, conv0_ref, FFN_CONV_W, cols)
        for r0 in range(0, seq.rows, CONV_ROWS):
            u_a, u_b = (_conv_piece(ucat_ref, cw_ref, cb_ref, r0, CONV_ROWS, cols, FFN_CONV_W)
                        for cols in halves)
            act_sc[pl.ds(r0, CONV_ROWS), c0:c0 + CONV_COLS] = (_gelu_tanh(u_a) * u_b).astype(bf16)
    _conv_state_out(seq, ucat_ref, convout_ref, FFN_CONV_W)
    y = x + jnp.dot(act_sc[...], wd_ref[...], preferred_element_type=f32)
    if final_norm:
        y = y * lax.rsqrt(jnp.mean(y * y, axis=-1, keepdims=True) + EPS) * gf_ref[...]
    o_ref[...] = y


def _ffn(x, g, wu, cw, cb, wd, gf, conv0, layer, seq, final_norm):
    m = x.shape[0]
    r = seq.rows
    row = lambda w: pl.BlockSpec((r, w), lambda i: (i, 0))

    def lw(shape, **kw):
        n = len(shape)
        return pl.BlockSpec((None,) + shape, lambda i: (layer,) + (0,) * n, **kw)

    once = dict(pipeline_mode=pl.Buffered(1))
    in_specs = [row(D_MODEL), lw((1, D_MODEL)), lw((D_MODEL, 2 * D_FF), **once),
                lw((FFN_CONV_W, 2 * D_FF)), lw((1, 2 * D_FF)), lw((D_FF, D_MODEL), **once),
                pl.BlockSpec((1, D_MODEL), lambda i: (0, 0))]
    args = [x, g, wu, cw, cb, wd, gf]
    tail = (FFN_CONV_W - 1, 2 * D_FF)
    if seq.has_state:
        in_specs.append(seq.in_state_spec(tail, layer))
        args.append(conv0)
    return pl.pallas_call(
        functools.partial(_ffn_kernel, seq=seq, final_norm=final_norm),
        grid=(seq.n_tiles,),
        in_specs=in_specs,
        out_specs=[row(D_MODEL), seq.state_spec(tail)],
        out_shape=[jax.ShapeDtypeStruct((m, D_MODEL), f32),
                   jax.ShapeDtypeStruct((seq.batch,) + tail, f32)],
        scratch_shapes=[pltpu.VMEM((r + SUBLANES, 2 * D_FF), f32), pltpu.VMEM((r, D_FF), bf16)],
        compiler_params=_params(1),
        name="ffn",
    )(*args)


def _head_pad(a):
    lead = a.shape[:-1]
    a = a.reshape(lead + (GLA_HEADS, GLA_DK))
    a = jnp.pad(a, [(0, 0)] * len(lead) + [(0, 0), (0, LANES - GLA_DK)])
    return a.reshape(lead + (GLA_HEADS * LANES,))


def _lane_pad(a, width):
    return jnp.pad(a, [(0, 0)] * (a.ndim - 1) + [(0, width - a.shape[-1])])


def _permute_in_cols(a):
    sizes = (256, 256, 512, 512, GLA_RANK, 512, M_HEADS, M_HEADS, 512, 512, 512, 3 * D_MODEL)
    parts = []
    off = 0
    for s in sizes:
        parts.append(a[..., off:off + s])
        off += s
    gq, gk, gv, gg, ga, mu, mi, mf, mo, rx, ry, gates = parts
    out = jnp.concatenate(
        [gates, _head_pad(gq), _head_pad(gk), gv, gg, mu, mo, rx, ry,
         _lane_pad(ga, LANES), _lane_pad(jnp.concatenate([mi, mf], axis=-1), LANES)], axis=-1)
    return _lane_pad(out, D_IN_PAD)


def _block_diag(w):
    d, nb, bs, _ = w.shape
    eye = jnp.eye(nb, dtype=w.dtype)
    return jnp.einsum('lnde,nm->lndme', w, eye).reshape(d, nb * bs, nb * bs)


def _run_group(x, seq_mix, seq_ffn, states, wts, tm):
    per_layer = []
    for l in range(DEPTH):
        proj = _in_proj(x, wts['norm_mix_g'], wts['w_in'], wts['b_in'], l, tm=tm, tn=1536)
        st = states
        o_g, s_new = _gla(proj, wts['w_gla_a2'], wts['b_gla_a2'], wts['gla_norm_g'],
                          st[0] if st else None, l, seq_mix)
        o_m, c_new, n_new, m_new, mconv_new = _mlstm(
            proj, wts['mlstm_conv_w'], wts['mlstm_conv_b'], wts['w_mlstm_q'], wts['w_mlstm_k'],
            wts['w_mlstm_v'], wts['mlstm_norm_g'], st[1:5] if st else None, l, seq_mix)
        o_r, h_new, rconv_new = _rglru(
            proj, wts['rglru_conv_w'], wts['rglru_conv_b'], wts['w_rglru_r'], wts['b_rglru_r'],
            wts['w_rglru_i'], wts['b_rglru_i'], wts['rglru_lambda'], st[5:7] if st else None,
            l, seq_mix)
        x = _merge(proj, o_g, o_m, o_r, x, wts['w_branch'], wts['w_out'], l, tm=min(tm, 512))
        x, fconv_new = _ffn(x, wts['norm_ffn_g'], wts['w_up'], wts['ffn_conv_w'],
                            wts['ffn_conv_b'], wts['w_down'], wts['norm_f_g'],
                            st[7] if st else None, l, seq_ffn, final_norm=(l == DEPTH - 1))
        per_layer.append([s_new, c_new, n_new, m_new, mconv_new, h_new, rconv_new, fconv_new])
    new_states = [jnp.stack([st[j] for st in per_layer]) for j in range(8)]
    return x, new_states


def kernel(x_prompt, x_sample, state_gla_S, state_mlstm_C, state_mlstm_n, state_mlstm_m,
           state_mlstm_conv, state_rglru_h, state_rglru_conv, state_ffn_conv,
           norm_mix_g, w_in, b_in, w_gla_a2, b_gla_a2, gla_norm_g,
           mlstm_conv_w, mlstm_conv_b, w_mlstm_q, w_mlstm_k, w_mlstm_v, mlstm_norm_g,
           rglru_conv_w, rglru_conv_b, w_rglru_r, b_rglru_r, w_rglru_i, b_rglru_i, rglru_lambda,
           w_branch, w_out, norm_ffn_g, w_up, ffn_conv_w, ffn_conv_b, w_down, norm_f_g):
    row = lambda a: a[:, None, :]
    wts = {
        'norm_mix_g': row(norm_mix_g),
        'w_in': _permute_in_cols(w_in).astype(bf16),
        'b_in': row(_permute_in_cols(b_in)),
        'w_gla_a2': _head_pad(jnp.pad(w_gla_a2, ((0, 0), (0, LANES - GLA_RANK), (0, 0)))).astype(bf16),
        'b_gla_a2': row(_head_pad(b_gla_a2)),
        'gla_norm_g': row(gla_norm_g),
        'mlstm_conv_w': mlstm_conv_w, 'mlstm_conv_b': row(mlstm_conv_b),
        'w_mlstm_q': w_mlstm_q.astype(bf16), 'w_mlstm_k': w_mlstm_k.astype(bf16),
        'w_mlstm_v': w_mlstm_v.astype(bf16), 'mlstm_norm_g': row(mlstm_norm_g),
        'rglru_conv_w': rglru_conv_w, 'rglru_conv_b': row(rglru_conv_b),
        'w_rglru_r': _block_diag(w_rglru_r).astype(bf16), 'b_rglru_r': row(b_rglru_r),
        'w_rglru_i': _block_diag(w_rglru_i).astype(bf16), 'b_rglru_i': row(b_rglru_i),
        'rglru_lambda': row(rglru_lambda),
        'w_branch': w_branch.astype(bf16), 'w_out': w_out.astype(bf16),
        'norm_ffn_g': row(norm_ffn_g), 'w_up': w_up.astype(bf16),
        'ffn_conv_w': ffn_conv_w, 'ffn_conv_b': row(ffn_conv_b),
        'w_down': w_down.astype(bf16), 'norm_f_g': norm_f_g[None, :],
    }

    bp, lp, _ = x_prompt.shape
    seq_p_mix = _Seq(bp, lp, lp, 512, has_state=False)
    seq_p_ffn = _Seq(bp, lp, lp, 256, has_state=False)
    y_p, p_states = _run_group(x_prompt.reshape(bp * lp, D_MODEL), seq_p_mix, seq_p_ffn,
                               None, wts, tm=1024)
    y_prompt = y_p.reshape(bp, lp, D_MODEL)

    bs, ls, _ = x_sample.shape
    xs = jnp.pad(x_sample, ((0, 0), (0, SUBLANES - ls), (0, 0))).reshape(bs * SUBLANES, D_MODEL)
    seq_s = _Seq(bs, SUBLANES, ls, 128, has_state=True)
    s_in = [state_gla_S, state_mlstm_C, state_mlstm_n, state_mlstm_m[:, :, None, :],
            state_mlstm_conv, state_rglru_h[:, :, None, :], state_rglru_conv, state_ffn_conv]
    y_s, s_states = _run_group(xs, seq_s, seq_s, s_in, wts, tm=1024)
    y_sample = y_s.reshape(bs, SUBLANES, D_MODEL)[:, :ls]

    def unpack(st):
        s, c, n, m, mconv, h, rconv, fconv = st
        return (s, c, n, m[:, :, 0, :], mconv, h[:, :, 0, :], rconv, fconv)

    return (y_prompt, y_sample) + unpack(p_states) + unpack(s_states)
```

```python
import functools

import jax
import jax.numpy as jnp
from jax import lax
from jax.experimental import pallas as pl
from jax.experimental.pallas import tpu as pltpu

f32 = jnp.float32
bf16 = jnp.bfloat16

D_MODEL = 1024
DEPTH = 4
GLA_HEADS = 4
GLA_DK = 64
GLA_DV = 128
GLA_RANK = 16
GLA_TAU = 16.0
M_HEADS = 4
M_DH = 128
M_WIDTH = 512
CONV_W = 4
R_WIDTH = 512
R_BLOCKS = 8
R_BS = 64
R_C = 8.0
D_FF = 2816
FFN_CONV_W = 3
EPS = 1e-6

LANES = 128
SUBLANES = 8
GLA_CHUNK = 64
MLSTM_CHUNK = 256
SHORT_UNROLL = 4
CONV_ROWS = 64
CONV_COLS = 256
VMEM_LIMIT = 52 * 1024 * 1024

COL_GATES = 0
COL_GQ = 3072
COL_GK = 3584
COL_GV = 4096
COL_GG = 4608
COL_MU = 5120
COL_MO = 5632
COL_RX = 6144
COL_RY = 6656
COL_GA = 7168
COL_MIF = 7296
D_IN_PAD = 7680


def _log_sigmoid(x):
    return jnp.minimum(x, 0.0) - jnp.log(1.0 + jnp.exp(-jnp.abs(x)))


def _silu(x):
    return x * jax.nn.sigmoid(x)


def _gelu_tanh(x):
    hx = 0.5 * x
    return hx + hx * jnp.tanh(x * (0.7978845608028654 + (0.7978845608028654 * 0.044715) * (x * x)))


def _head_rmsnorm(o, g):
    return o * lax.rsqrt(jnp.mean(o * o, axis=-1, keepdims=True) + EPS) * g


def _dot(a, b):
    return jnp.dot(a.astype(bf16), b.astype(bf16), preferred_element_type=f32)


def _dot_nt(a, b):
    return lax.dot_general(a.astype(bf16), b.astype(bf16), (((1,), (1,)), ((), ())),
                           preferred_element_type=f32)


def _dot_tn(a, b):
    return lax.dot_general(a.astype(bf16), b.astype(bf16), (((0,), (0,)), ((), ())),
                           preferred_element_type=f32)


def _bf16_parts(x):
    p0 = x.astype(bf16).astype(f32)
    r0 = x - p0
    p1 = r0.astype(bf16).astype(f32)
    p2 = (r0 - p1).astype(bf16).astype(f32)
    return p0, p1, p2


def _cumsum_rows(x):
    c, n = x.shape
    if c == SUBLANES:
        sub = lax.broadcasted_iota(jnp.int32, x.shape, 0)
        for d in (1, 2, 4):
            x = x + jnp.where(sub >= d, pltpu.roll(x, d, 0), 0.0)
        return x
    y = _dot(_tri(c).astype(f32), jnp.concatenate(_bf16_parts(x), axis=1))
    return (y[:, :n] + y[:, n:2 * n]) + y[:, 2 * n:]


def _first_cols_as_rows(x):
    c = x.shape[0]
    cp = max(c, LANES)
    sel = (lax.broadcasted_iota(jnp.int32, (SUBLANES, LANES), 0)
           == lax.broadcasted_iota(jnp.int32, (SUBLANES, LANES), 1)).astype(f32)
    pad = [jnp.zeros((cp - c, LANES), f32)] if cp > c else []
    stacked = jnp.concatenate([piece for p in _bf16_parts(x) for piece in [p] + pad], axis=0)
    y = _dot_nt(sel, stacked)
    return (y[:, 0:c] + y[:, cp:cp + c]) + y[:, 2 * cp:2 * cp + c]


def _tri(c):
    r = lax.broadcasted_iota(jnp.int32, (c, c), 0)
    s = lax.broadcasted_iota(jnp.int32, (c, c), 1)
    return r >= s


def _stack_heads(x):
    lane = lax.broadcasted_iota(jnp.int32, (1, x.shape[1]), 1)
    blocks = [jnp.where((lane >= h * LANES) & (lane < (h + 1) * LANES), x, 0.0)
              for h in range(x.shape[1] // LANES)]
    return jnp.concatenate(blocks, axis=0)


def _split_heads(x):
    return jnp.concatenate([x[:, h * LANES:(h + 1) * LANES] for h in range(x.shape[1] // LANES)],
                           axis=0)


def _params(n_axes):
    return pltpu.CompilerParams(dimension_semantics=("arbitrary",) * n_axes,
                                vmem_limit_bytes=VMEM_LIMIT)


def _in_proj_kernel(x_ref, g_ref, w_ref, b_ref, o_ref, xn_ref):
    @pl.when(pl.program_id(1) == 0)
    def _():
        x = x_ref[...]
        ms = jnp.mean(x * x, axis=-1, keepdims=True)
        xn_ref[...] = (x * lax.rsqrt(ms + EPS) * g_ref[...]).astype(bf16)

    o_ref[...] = jnp.dot(xn_ref[...], w_ref[...], preferred_element_type=f32) + b_ref[...]


def _in_proj(x, g, w, b, layer, *, tm, tn):
    m, d = x.shape
    n = w.shape[-1]
    return pl.pallas_call(
        _in_proj_kernel,
        grid=(m // tm, n // tn),
        in_specs=[pl.BlockSpec((tm, d), lambda i, j: (i, 0)),
                  pl.BlockSpec((None, 1, d), lambda i, j: (layer, 0, 0)),
                  pl.BlockSpec((None, d, tn), lambda i, j: (layer, 0, j)),
                  pl.BlockSpec((None, 1, tn), lambda i, j: (layer, 0, j))],
        out_specs=pl.BlockSpec((tm, tn), lambda i, j: (i, j)),
        out_shape=jax.ShapeDtypeStruct((m, n), f32),
        scratch_shapes=[pltpu.VMEM((tm, d), bf16)],
        compiler_params=_params(2),
        name="in_proj",
    )(x, g, w, b)


class _Seq:
    def __init__(self, batch, seq_pad, seq_real, rows, has_state):
        self.batch = batch
        self.seq_pad = seq_pad
        self.seq_real = seq_real
        self.rows = rows
        self.has_state = has_state
        self.short = seq_pad <= rows
        if self.short:
            assert rows % seq_pad == 0 and seq_pad == SUBLANES
            self.seqs_per_tile = rows // seq_pad
            self.tiles_per_seq = 1
            self.gla_chunk = self.mlstm_chunk = seq_pad
        else:
            assert seq_pad % rows == 0 and seq_pad == seq_real
            self.seqs_per_tile = 1
            self.tiles_per_seq = seq_pad // rows
            self.gla_chunk = GLA_CHUNK
            self.mlstm_chunk = min(MLSTM_CHUNK, rows)
        self.n_tiles = batch * seq_pad // rows

    def state_spec(self, tail):
        nt = len(tail)
        tps = self.tiles_per_seq
        return pl.BlockSpec((self.seqs_per_tile,) + tuple(tail),
                            lambda i: (i // tps,) + (0,) * nt)

    def in_state_spec(self, tail, layer):
        nt = len(tail)
        return pl.BlockSpec((None, self.seqs_per_tile) + tuple(tail),
                            lambda i: (layer, i) + (0,) * nt)


def _conv_carry(seq, ucat_ref):
    if seq.short:
        return
    r = seq.rows
    i = pl.program_id(0)
    width = ucat_ref.shape[-1]

    @pl.when(i % seq.tiles_per_seq == 0)
    def _():
        ucat_ref[pl.ds(0, SUBLANES), :] = jnp.zeros((SUBLANES, width), f32)

    @pl.when(i % seq.tiles_per_seq != 0)
    def _():
        ucat_ref[pl.ds(0, SUBLANES), :] = ucat_ref[pl.ds(r, SUBLANES), :]


def _conv_store(seq, ucat_ref, u, state_ref, width, cols):
    ucat_ref[pl.ds(SUBLANES, seq.rows), cols] = u
    if seq.short:
        for s in range(seq.seqs_per_tile):
            lo = SUBLANES + s * seq.seq_pad - (width - 1)
            if seq.has_state:
                ucat_ref[pl.ds(lo, width - 1), cols] = state_ref[s, :, cols]
            else:
                ucat_ref[pl.ds(lo, width - 1), cols] = jnp.zeros((width - 1, u.shape[-1]), f32)


def _conv_piece(ucat_ref, w_ref, b_ref, r0, rows, cols, width):
    u = ucat_ref[pl.ds(r0, rows + SUBLANES), cols]
    s = u * w_ref[0:1, cols]
    for j in range(1, width - 1):
        s = u * w_ref[j:j + 1, cols] + pltpu.roll(s, 1, 0)
    y = (b_ref[:, cols] + u * w_ref[width - 1:width, cols]) + pltpu.roll(s, 1, 0)
    return y[SUBLANES:, :]


def _conv_state_out(seq, ucat_ref, out_ref, width):
    r = seq.rows
    i = pl.program_id(0)
    if seq.short:
        for s in range(seq.seqs_per_tile):
            lo = SUBLANES + s * seq.seq_pad + seq.seq_real - (width - 1)
            out_ref[s] = ucat_ref[pl.ds(lo, width - 1), :]
    else:
        @pl.when(i % seq.tiles_per_seq == seq.tiles_per_seq - 1)
        def _():
            out_ref[0] = ucat_ref[pl.ds(SUBLANES + r - (width - 1), width - 1), :]


def _gla_kernel(*refs, seq):
    if seq.has_state:
        (q_ref, k_ref, v_ref, g_ref, a_ref, wa_ref, ba_ref, gn_ref, s0_ref,
         o_ref, sout_ref, st_ref, lg_sc) = refs
    else:
        (q_ref, k_ref, v_ref, g_ref, a_ref, wa_ref, ba_ref, gn_ref,
         o_ref, sout_ref, st_ref, lg_sc) = refs
        s0_ref = None
    c = seq.gla_chunk
    i = pl.program_id(0)
    heads = range(GLA_HEADS)
    mask4 = ((lax.broadcasted_iota(jnp.int32, (GLA_HEADS * c, c), 0) & (c - 1))
             >= lax.broadcasted_iota(jnp.int32, (GLA_HEADS * c, c), 1))
    zpad = jnp.zeros((LANES - GLA_DK, GLA_DV), f32)

    for r0 in range(0, seq.rows, LANES):
        rows = pl.ds(r0, LANES)
        lg_sc[rows, :] = _log_sigmoid(_dot(a_ref[rows, :], wa_ref[...]) + ba_ref[...]) / GLA_TAU

    def chunk(rows, st):
        q = q_ref[rows, :] * (GLA_DK ** -0.5)
        k = k_ref[rows, :]
        v = v_ref[rows, :]
        g = g_ref[rows, :]
        lg = lg_sc[rows, :]
        if seq.seq_real < c:
            valid = lax.broadcasted_iota(jnp.int32, (c, 1), 0) < seq.seq_real
            lg = jnp.where(valid, lg, 0.0)
            k = jnp.where(valid, k, 0.0)
        bcum = _cumsum_rows(lg)
        blast = bcum[c - 1:c, :]
        q_d = q * jnp.exp(bcum)
        k_d = k * jnp.exp(-bcum)
        k_e = k * jnp.exp(blast - bcum)
        qd_st = _stack_heads(q_d).astype(bf16)
        a_st = jnp.where(mask4, _dot_nt(qd_st, k_d), 0.0)
        o_intra = _dot(a_st, v)
        o_inter = _dot_nt(qd_st, st)
        new_st = jnp.exp(blast) * st + _dot_tn(_split_heads(v), _stack_heads(k_e))
        for h in heads:
            sl = slice(h * LANES, (h + 1) * LANES)
            o_h = o_intra[h * c:(h + 1) * c, sl] + o_inter[h * c:(h + 1) * c, :]
            o_ref[rows, sl] = _head_rmsnorm(o_h, gn_ref[:, sl]) * _silu(g[:, sl])
        return new_st

    def state_out(idx, st):
        for h in heads:
            sout_ref[idx, h] = st[:, h * LANES:(h + 1) * LANES].T[0:GLA_DK, :]

    if seq.short:
        def body(ci, carry):
            rows = pl.ds(pl.multiple_of(ci * c, c), c)
            if seq.has_state:
                st = jnp.concatenate(
                    [jnp.concatenate([s0_ref[ci, h], zpad], axis=0).T for h in heads], axis=1)
            else:
                st = jnp.zeros((GLA_DV, GLA_HEADS * LANES), f32)
            state_out(ci, chunk(rows, st))
            return carry

        lax.fori_loop(0, seq.rows // c, body, 0, unroll=SHORT_UNROLL)
    else:
        @pl.when(i % seq.tiles_per_seq == 0)
        def _():
            st_ref[...] = jnp.zeros_like(st_ref)

        st = st_ref[...]
        for ci in range(seq.rows // c):
            st = chunk(pl.ds(ci * c, c), st)
        st_ref[...] = st

        @pl.when(i % seq.tiles_per_seq == seq.tiles_per_seq - 1)
        def _():
            state_out(0, st)


def _gla(proj, wa, ba, gn, s0, layer, seq):
    m = proj.shape[0]
    r = seq.rows

    def col(cb, w):
        return pl.BlockSpec((r, w), lambda i: (i, cb))

    in_specs = [col(COL_GQ // 512, 512), col(COL_GK // 512, 512), col(COL_GV // 512, 512),
                col(COL_GG // 512, 512), col(COL_GA // LANES, LANES),
                pl.BlockSpec((None, LANES, 512), lambda i: (layer, 0, 0)),
                pl.BlockSpec((None, 1, 512), lambda i: (layer, 0, 0)),
                pl.BlockSpec((None, 1, 512), lambda i: (layer, 0, 0))]
    args = [proj, proj, proj, proj, proj, wa, ba, gn]
    tail = (GLA_HEADS, GLA_DK, GLA_DV)
    if seq.has_state:
        in_specs.append(seq.in_state_spec(tail, layer))
        args.append(s0)
    return pl.pallas_call(
        functools.partial(_gla_kernel, seq=seq),
        grid=(seq.n_tiles,),
        in_specs=in_specs,
        out_specs=[pl.BlockSpec((r, 512), lambda i: (i, 0)), seq.state_spec(tail)],
        out_shape=[jax.ShapeDtypeStruct((m, 512), f32),
                   jax.ShapeDtypeStruct((seq.batch,) + tail, f32)],
        scratch_shapes=[pltpu.VMEM((GLA_DV, GLA_HEADS * LANES), f32), pltpu.VMEM((r, 512), f32)],
        compiler_params=_params(1),
        name="gla",
    )(*args)


def _mlstm_kernel(*refs, seq):
    n_in = 9
    (mu_ref, mo_ref, mif_ref, cw_ref, cb_ref, wq_ref, wk_ref, wv_ref, gn_ref) = refs[:n_in]
    if seq.has_state:
        c0_ref, n0_ref, m0_ref, conv0_ref = refs[n_in:n_in + 4]
        rest = refs[n_in + 4:]
    else:
        c0_ref = n0_ref = m0_ref = conv0_ref = None
        rest = refs[n_in:]
    (o_ref, cout_ref, nout_ref, mout_ref, convout_ref,
     ucat_ref, mcv_sc, q_sc, k_sc, v_sc, c_st, n_st, m_st) = rest
    c = seq.mlstm_chunk
    i = pl.program_id(0)
    heads = range(M_HEADS)
    mask = _tri(c)
    lane = lax.broadcasted_iota(jnp.int32, (c, LANES), 1)
    lane_h = lax.broadcasted_iota(jnp.int32, (1, M_HEADS), 1)

    mu = mu_ref[...]
    _conv_carry(seq, ucat_ref)
    _conv_store(seq, ucat_ref, mu, conv0_ref, CONV_W, slice(0, M_WIDTH))
    _conv_state_out(seq, ucat_ref, convout_ref, CONV_W)
    for r0 in range(0, seq.rows, CONV_ROWS):
        for c0 in range(0, M_WIDTH, CONV_COLS):
            cols = slice(c0, c0 + CONV_COLS)
            mcv_sc[pl.ds(r0, CONV_ROWS), cols] = _silu(
                _conv_piece(ucat_ref, cw_ref, cb_ref, r0, CONV_ROWS, cols, CONV_W)).astype(bf16)
    for h in heads:
        sl = slice(h * M_DH, (h + 1) * M_DH)
        q_sc[:, sl] = _dot(mcv_sc[:, sl], wq_ref[h])
        k_sc[:, sl] = _dot(mcv_sc[:, sl], wk_ref[h]) * (M_DH ** -0.5)
        v_sc[:, sl] = _dot(mu[:, sl], wv_ref[h])

    def chunk(rows, c_all, n_all, m_prevs):
        gi = mif_ref[rows, :]
        lf = _log_sigmoid(gi)
        if seq.seq_real < c:
            valid_c = lax.broadcasted_iota(jnp.int32, (c, 1), 0) < seq.seq_real
            valid_r = lax.broadcasted_iota(jnp.int32, (1, c), 1) < seq.seq_real
            lf = jnp.where(valid_c, lf, 0.0)
        fcum = _cumsum_rows(lf)
        xt = _first_cols_as_rows(jnp.where(lane < M_HEADS, gi, fcum))
        q = q_sc[rows, :]
        k = k_sc[rows, :]
        v = v_sc[rows, :]
        mo = mo_ref[rows, :]
        q_st = _stack_heads(q).astype(bf16)
        qk_st = _dot_nt(q_st, k)
        inter_st = _dot_nt(q_st, c_all)
        p_blocks, ws_blocks, per_head = [], [], []
        for h in heads:
            sl = slice(h * M_DH, (h + 1) * M_DH)
            f_col = fcum[:, M_HEADS + h:M_HEADS + h + 1]
            i_col = gi[:, h:h + 1]
            f_row = xt[M_HEADS + h:M_HEADS + h + 1, :]
            i_row = xt[h:h + 1, :]
            if seq.seq_real < c:
                i_col = jnp.where(valid_c, i_col, -jnp.inf)
                i_row = jnp.where(valid_r, i_row, -jnp.inf)
            dm = jnp.where(mask, f_col - f_row + i_row, -jnp.inf)
            b = f_col + m_prevs[h]
            mt = jnp.maximum(b, jnp.max(dm, axis=-1, keepdims=True))
            w_inter = jnp.exp(b - mt)
            p = jnp.exp(dm - mt) * qk_st[h * c:(h + 1) * c, :]
            den = (w_inter * jnp.sum(q[:, sl] * n_all[:, sl], axis=-1, keepdims=True)
                   + jnp.sum(p, axis=-1, keepdims=True))
            m_new = mt[c - 1:c, :]
            dec = jnp.exp(b[c - 1:c, :] - m_new)
            ws = jnp.exp(f_col[c - 1:c, :] - f_col + i_col - m_new)
            p_blocks.append(p)
            ws_blocks.append(ws)
            per_head.append((w_inter, den, mt, m_new, dec))
        pv_st = _dot(jnp.concatenate(p_blocks, axis=0), v)
        wsv_st = jnp.concatenate([ws_blocks[h] * v[:, h * M_DH:(h + 1) * M_DH] for h in heads],
                                 axis=0)
        dec_all = jnp.concatenate([jnp.broadcast_to(per_head[h][4], (1, M_DH)) for h in heads],
                                  axis=1)
        new_c = dec_all * c_all + _dot_tn(wsv_st, _stack_heads(k))
        new_n = dec_all * n_all + jnp.concatenate(
            [jnp.sum(ws_blocks[h] * k[:, h * M_DH:(h + 1) * M_DH], axis=0, keepdims=True)
             for h in heads], axis=1)
        for h in heads:
            sl = slice(h * M_DH, (h + 1) * M_DH)
            w_inter, den, mt, _, _ = per_head[h]
            num = w_inter * inter_st[h * c:(h + 1) * c, :] + pv_st[h * c:(h + 1) * c, sl]
            hh = num / jnp.maximum(jnp.abs(den), jnp.exp(-mt))
            o_ref[rows, sl] = _head_rmsnorm(hh, gn_ref[:, sl]) * jax.nn.sigmoid(mo[:, sl])
        return new_c, new_n, [per_head[h][3] for h in heads]

    def store_state(idx, c_all, n_all, m_prevs):
        m_row = jnp.zeros((1, M_HEADS), f32)
        for h in heads:
            sl = slice(h * M_DH, (h + 1) * M_DH)
            cout_ref[idx, h] = c_all[:, sl]
            nout_ref[idx, pl.ds(h, 1), :] = n_all[:, sl]
            m_row = jnp.where(lane_h == h, m_prevs[h], m_row)
        mout_ref[idx] = m_row

    if seq.short:
        def body(ci, carry):
            rows = pl.ds(pl.multiple_of(ci * c, c), c)
            if seq.has_state:
                n0 = n0_ref[ci]
                m0 = m0_ref[ci]
                state = (jnp.concatenate([c0_ref[ci, h] for h in heads], axis=1),
                         jnp.concatenate([n0[h:h + 1, :] for h in heads], axis=1),
                         [m0[:, h:h + 1] for h in heads])
            else:
                state = (jnp.zeros((M_DH, M_WIDTH), f32), jnp.zeros((1, M_WIDTH), f32),
                         [jnp.zeros((1, 1), f32)] * M_HEADS)
            store_state(ci, *chunk(rows, *state))
            return carry

        lax.fori_loop(0, seq.rows // c, body, 0, unroll=SHORT_UNROLL)
    else:
        @pl.when(i % seq.tiles_per_seq == 0)
        def _():
            c_st[...] = jnp.zeros_like(c_st)
            n_st[...] = jnp.zeros_like(n_st)
            m_st[...] = jnp.zeros_like(m_st)

        state = (c_st[...], n_st[0:1, :], [m_st[h:h + 1, 0:1] for h in heads])
        for ci in range(seq.rows // c):
            state = chunk(pl.ds(ci * c, c), *state)
        c_st[...] = state[0]
        n_st[0:1, :] = state[1]
        for h in heads:
            m_st[h:h + 1, :] = jnp.broadcast_to(state[2][h], (1, LANES))

        @pl.when(i % seq.tiles_per_seq == seq.tiles_per_seq - 1)
        def _():
            store_state(0, *state)


def _mlstm(proj, cw, cb, wq, wk, wv, gn, states, layer, seq):
    m = proj.shape[0]
    r = seq.rows

    def col(cb_, w):
        return pl.BlockSpec((r, w), lambda i: (i, cb_))

    def lw(shape):
        n = len(shape)
        return pl.BlockSpec((None,) + shape, lambda i: (layer,) + (0,) * n)

    in_specs = [col(COL_MU // 512, 512), col(COL_MO // 512, 512), col(COL_MIF // LANES, LANES),
                lw((CONV_W, M_WIDTH)), lw((1, M_WIDTH)),
                lw((M_HEADS, M_DH, M_DH)), lw((M_HEADS, M_DH, M_DH)), lw((M_HEADS, M_DH, M_DH)),
                lw((1, M_WIDTH))]
    args = [proj, proj, proj, cw, cb, wq, wk, wv, gn]
    tails = [(M_HEADS, M_DH, M_DH), (M_HEADS, M_DH), (1, M_HEADS), (CONV_W - 1, M_WIDTH)]
    if seq.has_state:
        in_specs += [seq.in_state_spec(t, layer) for t in tails]
        args += list(states)
    return pl.pallas_call(
        functools.partial(_mlstm_kernel, seq=seq),
        grid=(seq.n_tiles,),
        in_specs=in_specs,
        out_specs=[pl.BlockSpec((r, M_WIDTH), lambda i: (i, 0))]
        + [seq.state_spec(t) for t in tails],
        out_shape=[jax.ShapeDtypeStruct((m, M_WIDTH), f32)]
        + [jax.ShapeDtypeStruct((seq.batch,) + t, f32) for t in tails],
        scratch_shapes=[pltpu.VMEM((r + SUBLANES, M_WIDTH), f32), pltpu.VMEM((r, M_WIDTH), bf16),
                        pltpu.VMEM((r, M_WIDTH), f32), pltpu.VMEM((r, M_WIDTH), f32),
                        pltpu.VMEM((r, M_WIDTH), f32),
                        pltpu.VMEM((M_DH, M_WIDTH), f32),
                        pltpu.VMEM((SUBLANES, M_WIDTH), f32), pltpu.VMEM((SUBLANES, LANES), f32)],
        compiler_params=_params(1),
        name="mlstm",
    )(*args)


def _rglru_kernel(*refs, seq):
    n_in = 9
    (rx_ref, ry_ref, cw_ref, cb_ref, wr_ref, br_ref, wi_ref, bi_ref, lam_ref) = refs[:n_in]
    if seq.has_state:
        h0_ref, conv0_ref = refs[n_in:n_in + 2]
        rest = refs[n_in + 2:]
    else:
        h0_ref = conv0_ref = None
        rest = refs[n_in:]
    o_ref, hout_ref, convout_ref, ucat_ref, xc_sc, a_sc, b_sc, h_st = rest
    r = seq.rows
    i = pl.program_id(0)

    _conv_carry(seq, ucat_ref)
    _conv_store(seq, ucat_ref, rx_ref[...], conv0_ref, CONV_W, slice(0, R_WIDTH))
    _conv_state_out(seq, ucat_ref, convout_ref, CONV_W)
    pieces = [(r0, slice(c0, c0 + CONV_COLS))
              for r0 in range(0, r, CONV_ROWS) for c0 in range(0, R_WIDTH, CONV_COLS)]
    for r0, cols in pieces:
        xc_sc[pl.ds(r0, CONV_ROWS), cols] = _conv_piece(ucat_ref, cw_ref, cb_ref, r0, CONV_ROWS,
                                                        cols, CONV_W)
    xc_bf = xc_sc[...].astype(bf16)
    a_sc[...] = jnp.dot(xc_bf, wr_ref[...], preferred_element_type=f32) + br_ref[...]
    b_sc[...] = jnp.dot(xc_bf, wi_ref[...], preferred_element_type=f32) + bi_ref[...]
    log_lam = _log_sigmoid(lam_ref[...])
    for r0, cols in pieces:
        rows = pl.ds(r0, CONV_ROWS)
        log_a = R_C * jax.nn.sigmoid(a_sc[rows, cols]) * log_lam[:, cols]
        a = jnp.exp(log_a)
        gated_x = jax.nn.sigmoid(b_sc[rows, cols]) * xc_sc[rows, cols]
        a_sc[rows, cols] = a
        b_sc[rows, cols] = jnp.sqrt(-jnp.tanh(log_a) * (a * a + 1.0)) * gated_x

    if not seq.short:
        @pl.when(i % seq.tiles_per_seq == 0)
        def _():
            h_st[...] = jnp.zeros_like(h_st)

    sub = lax.broadcasted_iota(jnp.int32, (SUBLANES, R_WIDTH), 0)

    def body(gi, h_prev):
        rows = pl.ds(pl.multiple_of(gi * SUBLANES, SUBLANES), SUBLANES)
        ag = a_sc[rows, :]
        bg = b_sc[rows, :]
        for d in (1, 2, 4):
            keep = sub >= d
            bg = jnp.where(keep, ag * pltpu.roll(bg, d, 0) + bg, bg)
            ag = jnp.where(keep, ag * pltpu.roll(ag, d, 0), ag)
        if seq.short:
            if seq.has_state:
                h_in = h0_ref[gi]
            else:
                h_in = jnp.zeros((1, R_WIDTH), f32)
        else:
            h_in = h_prev
        hg = bg + ag * h_in
        o_ref[rows, :] = hg * _gelu_tanh(ry_ref[rows, :])
        if seq.short:
            hout_ref[gi] = hg[seq.seq_real - 1:seq.seq_real, :]
            return h_prev
        return hg[SUBLANES - 1:SUBLANES, :]

    h_last = lax.fori_loop(0, r // SUBLANES, body, h_st[0:1, :])

    if not seq.short:
        h_st[0:1, :] = h_last

        @pl.when(i % seq.tiles_per_seq == seq.tiles_per_seq - 1)
        def _():
            hout_ref[0] = h_last


def _rglru(proj, cw, cb, wr, br, wi, bi, lam, states, layer, seq):
    m = proj.shape[0]
    r = seq.rows

    def col(cb_, w):
        return pl.BlockSpec((r, w), lambda i: (i, cb_))

    def lw(shape):
        n = len(shape)
        return pl.BlockSpec((None,) + shape, lambda i: (layer,) + (0,) * n)

    in_specs = [col(COL_RX // 512, 512), col(COL_RY // 512, 512),
                lw((CONV_W, R_WIDTH)), lw((1, R_WIDTH)),
                lw((R_WIDTH, R_WIDTH)), lw((1, R_WIDTH)),
                lw((R_WIDTH, R_WIDTH)), lw((1, R_WIDTH)), lw((1, R_WIDTH))]
    args = [proj, proj, cw, cb, wr, br, wi, bi, lam]
    tails = [(1, R_WIDTH), (CONV_W - 1, R_WIDTH)]
    if seq.has_state:
        in_specs += [seq.in_state_spec(t, layer) for t in tails]
        args += list(states)
    return pl.pallas_call(
        functools.partial(_rglru_kernel, seq=seq),
        grid=(seq.n_tiles,),
        in_specs=in_specs,
        out_specs=[pl.BlockSpec((r, R_WIDTH), lambda i: (i, 0))]
        + [seq.state_spec(t) for t in tails],
        out_shape=[jax.ShapeDtypeStruct((m, R_WIDTH), f32)]
        + [jax.ShapeDtypeStruct((seq.batch,) + t, f32) for t in tails],
        scratch_shapes=[pltpu.VMEM((r + SUBLANES, R_WIDTH), f32), pltpu.VMEM((r, R_WIDTH), f32),
                        pltpu.VMEM((r, R_WIDTH), f32), pltpu.VMEM((r, R_WIDTH), f32),
                        pltpu.VMEM((SUBLANES, R_WIDTH), f32)],
        compiler_params=_params(1),
        name="rglru",
    )(*args)


def _merge_kernel(gates_ref, og_ref, om_ref, or_ref, x_ref, wb_ref, wo_ref, o_ref):
    mix = None
    for j, br_ref in enumerate((og_ref, om_ref, or_ref)):
        gate = jax.nn.sigmoid(gates_ref[:, j * D_MODEL:(j + 1) * D_MODEL])
        term = gate * _dot(br_ref[...], wb_ref[j * 512:(j + 1) * 512, :])
        mix = term if mix is None else mix + term
    o_ref[...] = x_ref[...] + _dot(mix, wo_ref[...])


def _merge(proj, o_g, o_m, o_r, x, wb, wo, layer, *, tm):
    m = x.shape[0]
    row = lambda w: pl.BlockSpec((tm, w), lambda i: (i, 0))
    return pl.pallas_call(
        _merge_kernel,
        grid=(m // tm,),
        in_specs=[row(3 * D_MODEL), row(512), row(512), row(512), row(D_MODEL),
                  pl.BlockSpec((None, 1536, D_MODEL), lambda i: (layer, 0, 0)),
                  pl.BlockSpec((None, D_MODEL, D_MODEL), lambda i: (layer, 0, 0))],
        out_specs=row(D_MODEL),
        out_shape=jax.ShapeDtypeStruct((m, D_MODEL), f32),
        compiler_params=_params(1),
        name="merge",
    )(proj, o_g, o_m, o_r, x, wb, wo)


def _ffn_kernel(*refs, seq, final_norm):
    n_in = 7
    x_ref, g_ref, wu_ref, cw_ref, cb_ref, wd_ref, gf_ref = refs[:n_in]
    if seq.has_state:
        conv0_ref = refs[n_in]
        rest = refs[n_in + 1:]
    else:
        conv0_ref = None
        rest = refs[n_in:]
    o_ref, convout_ref, ucat_ref, act_sc = rest
    x = x_ref[...]
    xn = (x * lax.rsqrt(jnp.mean(x * x, axis=-1, keepdims=True) + EPS) * g_ref[...]).astype(bf16)
    _conv_carry(seq, ucat_ref)
    for c0 in range(0, D_FF, CONV_COLS):
        halves = [slice(base + c0, base + c0 + CONV_COLS) for base in (0, D_FF)]
        for cols in halves:
            u = jnp.dot(xn, wu_ref[:, cols], preferred_element_type=f32)
            _conv_store(seq, ucat_ref, u, conv0_ref, FFN_CONV_W, cols)
        for r0 in range(0, seq.rows, CONV_ROWS):
            u_a, u_b = (_conv_piece(ucat_ref, cw_ref, cb_ref, r0, CONV_ROWS, cols, FFN_CONV_W)
                        for cols in halves)
            act_sc[pl.ds(r0, CONV_ROWS), c0:c0 + CONV_COLS] = (_gelu_tanh(u_a) * u_b).astype(bf16)
    _conv_state_out(seq, ucat_ref, convout_ref, FFN_CONV_W)
    y = x + jnp.dot(act_sc[...], wd_ref[...], preferred_element_type=f32)
    if final_norm:
        y = y * lax.rsqrt(jnp.mean(y * y, axis=-1, keepdims=True) + EPS) * gf_ref[...]
    o_ref[...] = y


def _ffn(x, g, wu, cw, cb, wd, gf, conv0, layer, seq, final_norm):
    m = x.shape[0]
    r = seq.rows
    row = lambda w: pl.BlockSpec((r, w), lambda i: (i, 0))

    def lw(shape, **kw):
        n = len(shape)
        return pl.BlockSpec((None,) + shape, lambda i: (layer,) + (0,) * n, **kw)

    once = dict(pipeline_mode=pl.Buffered(1))
    in_specs = [row(D_MODEL), lw((1, D_MODEL)), lw((D_MODEL, 2 * D_FF), **once),
                lw((FFN_CONV_W, 2 * D_FF)), lw((1, 2 * D_FF)), lw((D_FF, D_MODEL), **once),
                pl.BlockSpec((1, D_MODEL), lambda i: (0, 0))]
    args = [x, g, wu, cw, cb, wd, gf]
    tail = (FFN_CONV_W - 1, 2 * D_FF)
    if seq.has_state:
        in_specs.append(seq.in_state_spec(tail, layer))
        args.append(conv0)
    return pl.pallas_call(
        functools.partial(_ffn_kernel, seq=seq, final_norm=final_norm),
        grid=(seq.n_tiles,),
        in_specs=in_specs,
        out_specs=[row(D_MODEL), seq.state_spec(tail)],
        out_shape=[jax.ShapeDtypeStruct((m, D_MODEL), f32),
                   jax.ShapeDtypeStruct((seq.batch,) + tail, f32)],
        scratch_shapes=[pltpu.VMEM((r + SUBLANES, 2 * D_FF), f32), pltpu.VMEM((r, D_FF), bf16)],
        compiler_params=_params(1),
        name="ffn",
    )(*args)


def _head_pad(a):
    lead = a.shape[:-1]
    a = a.reshape(lead + (GLA_HEADS, GLA_DK))
    a = jnp.pad(a, [(0, 0)] * len(lead) + [(0, 0), (0, LANES - GLA_DK)])
    return a.reshape(lead + (GLA_HEADS * LANES,))


def _lane_pad(a, width):
    return jnp.pad(a, [(0, 0)] * (a.ndim - 1) + [(0, width - a.shape[-1])])


def _permute_in_cols(a):
    sizes = (256, 256, 512, 512, GLA_RANK, 512, M_HEADS, M_HEADS, 512, 512, 512, 3 * D_MODEL)
    parts = []
    off = 0
    for s in sizes:
        parts.append(a[..., off:off + s])
        off += s
    gq, gk, gv, gg, ga, mu, mi, mf, mo, rx, ry, gates = parts
    out = jnp.concatenate(
        [gates, _head_pad(gq), _head_pad(gk), gv, gg, mu, mo, rx, ry,
         _lane_pad(ga, LANES), _lane_pad(jnp.concatenate([mi, mf], axis=-1), LANES)], axis=-1)
    return _lane_pad(out, D_IN_PAD)


def _block_diag(w):
    d, nb, bs, _ = w.shape
    eye = jnp.eye(nb, dtype=w.dtype)
    return jnp.einsum('lnde,nm->lndme', w, eye).reshape(d, nb * bs, nb * bs)


def _run_group(x, seq_mix, seq_ffn, states, wts, tm):
    per_layer = []
    for l in range(DEPTH):
        proj = _in_proj(x, wts['norm_mix_g'], wts['w_in'], wts['b_in'], l, tm=tm, tn=1536)
        st = states
        o_g, s_new = _gla(proj, wts['w_gla_a2'], wts['b_gla_a2'], wts['gla_norm_g'],
                          st[0] if st else None, l, seq_mix)
        o_m, c_new, n_new, m_new, mconv_new = _mlstm(
            proj, wts['mlstm_conv_w'], wts['mlstm_conv_b'], wts['w_mlstm_q'], wts['w_mlstm_k'],
            wts['w_mlstm_v'], wts['mlstm_norm_g'], st[1:5] if st else None, l, seq_mix)
        o_r, h_new, rconv_new = _rglru(
            proj, wts['rglru_conv_w'], wts['rglru_conv_b'], wts['w_rglru_r'], wts['b_rglru_r'],
            wts['w_rglru_i'], wts['b_rglru_i'], wts['rglru_lambda'], st[5:7] if st else None,
            l, seq_mix)
        x = _merge(proj, o_g, o_m, o_r, x, wts['w_branch'], wts['w_out'], l, tm=min(tm, 512))
        x, fconv_new = _ffn(x, wts['norm_ffn_g'], wts['w_up'], wts['ffn_conv_w'],
                            wts['ffn_conv_b'], wts['w_down'], wts['norm_f_g'],
                            st[7] if st else None, l, seq_ffn, final_norm=(l == DEPTH - 1))
        per_layer.append([s_new, c_new, n_new, m_new, mconv_new, h_new, rconv_new, fconv_new])
    new_states = [jnp.stack([st[j] for st in per_layer]) for j in range(8)]
    return x, new_states


def kernel(x_prompt, x_sample, state_gla_S, state_mlstm_C, state_mlstm_n, state_mlstm_m,
           state_mlstm_conv, state_rglru_h, state_rglru_conv, state_ffn_conv,
           norm_mix_g, w_in, b_in, w_gla_a2, b_gla_a2, gla_norm_g,
           mlstm_conv_w, mlstm_conv_b, w_mlstm_q, w_mlstm_k, w_mlstm_v, mlstm_norm_g,
           rglru_conv_w, rglru_conv_b, w_rglru_r, b_rglru_r, w_rglru_i, b_rglru_i, rglru_lambda,
           w_branch, w_out, norm_ffn_g, w_up, ffn_conv_w, ffn_conv_b, w_down, norm_f_g):
    row = lambda a: a[:, None, :]
    wts = {
        'norm_mix_g': row(norm_mix_g),
        'w_in': _permute_in_cols(w_in).astype(bf16),
        'b_in': row(_permute_in_cols(b_in)),
        'w_gla_a2': _head_pad(jnp.pad(w_gla_a2, ((0, 0), (0, LANES - GLA_RANK), (0, 0)))).astype(bf16),
        'b_gla_a2': row(_head_pad(b_gla_a2)),
        'gla_norm_g': row(gla_norm_g),
        'mlstm_conv_w': mlstm_conv_w, 'mlstm_conv_b': row(mlstm_conv_b),
        'w_mlstm_q': w_mlstm_q.astype(bf16), 'w_mlstm_k': w_mlstm_k.astype(bf16),
        'w_mlstm_v': w_mlstm_v.astype(bf16), 'mlstm_norm_g': row(mlstm_norm_g),
        'rglru_conv_w': rglru_conv_w, 'rglru_conv_b': row(rglru_conv_b),
        'w_rglru_r': _block_diag(w_rglru_r).astype(bf16), 'b_rglru_r': row(b_rglru_r),
        'w_rglru_i': _block_diag(w_rglru_i).astype(bf16), 'b_rglru_i': row(b_rglru_i),
        'rglru_lambda': row(rglru_lambda),
        'w_branch': w_branch.astype(bf16), 'w_out': w_out.astype(bf16),
        'norm_ffn_g': row(norm_ffn_g), 'w_up': w_up.astype(bf16),
        'ffn_conv_w': ffn_conv_w, 'ffn_conv_b': row(ffn_conv_b),
        'w_down': w_down.astype(bf16), 'norm_f_g': norm_f_g[None, :],
    }

    bp, lp, _ = x_prompt.shape
    seq_p_mix = _Seq(bp, lp, lp, 512, has_state=False)
    seq_p_ffn = _Seq(bp, lp, lp, 256, has_state=False)
    y_p, p_states = _run_group(x_prompt.reshape(bp * lp, D_MODEL), seq_p_mix, seq_p_ffn,
                               None, wts, tm=1024)
    y_prompt = y_p.reshape(bp, lp, D_MODEL)

    bs, ls, _ = x_sample.shape
    xs = jnp.pad(x_sample, ((0, 0), (0, SUBLANES - ls), (0, 0))).reshape(bs * SUBLANES, D_MODEL)
    seq_s = _Seq(bs, SUBLANES, ls, 128, has_state=True)
    s_in = [state_gla_S, state_mlstm_C, state_mlstm_n, state_mlstm_m[:, :, None, :],
            state_mlstm_conv, state_rglru_h[:, :, None, :], state_rglru_conv, state_ffn_conv]
    y_s, s_states = _run_group(xs, seq_s, seq_s, s_in, wts, tm=1024)
    y_sample = y_s.reshape(bs, SUBLANES, D_MODEL)[:, :ls]

    def unpack(st):
        s, c, n, m, mconv, h, rconv, fconv = st
        return (s, c, n, m[:, :, 0, :], mconv, h[:, :, 0, :], rconv, fconv)

    return (y_prompt, y_sample) + unpack(p_states) + unpack(s_states)
```

```python
import functools

import jax
import jax.numpy as jnp
from jax import lax
from jax.experimental import pallas as pl
from jax.experimental.pallas import tpu as pltpu

f32 = jnp.float32
bf16 = jnp.bfloat16

D_MODEL = 1024
DEPTH = 4
GLA_HEADS = 4
GLA_DK = 64
GLA_DV = 128
GLA_RANK = 16
GLA_TAU = 16.0
M_HEADS = 4
M_DH = 128
M_WIDTH = 512
CONV_W = 4
R_WIDTH = 512
R_BLOCKS = 8
R_BS = 64
R_C = 8.0
D_FF = 2816
FFN_CONV_W = 3
EPS = 1e-6

LANES = 128
SUBLANES = 8
GLA_CHUNK = 64
MLSTM_CHUNK = 256
SHORT_UNROLL = 4
CONV_ROWS = 64
CONV_COLS = 256
FFN_GROUP = 1024
VMEM_LIMIT = 52 * 1024 * 1024

COL_GATES = 0
COL_GQ = 3072
COL_GK = 3584
COL_GV = 4096
COL_GG = 4608
COL_MU = 5120
COL_MO = 5632
COL_RX = 6144
COL_RY = 6656
COL_GA = 7168
COL_MIF = 7296
D_IN_PAD = 7680


def _log_sigmoid(x):
    return jnp.minimum(x, 0.0) - jnp.log(1.0 + jnp.exp(-jnp.abs(x)))


def _silu(x):
    return x * jax.nn.sigmoid(x)


def _gelu_tanh(x):
    hx = 0.5 * x
    return hx + hx * jnp.tanh(x * (0.7978845608028654 + (0.7978845608028654 * 0.044715) * (x * x)))


def _head_rmsnorm(o, g):
    return o * lax.rsqrt(jnp.mean(o * o, axis=-1, keepdims=True) + EPS) * g


def _dot(a, b):
    return jnp.dot(a.astype(bf16), b.astype(bf16), preferred_element_type=f32)


def _dot_nt(a, b):
    return lax.dot_general(a.astype(bf16), b.astype(bf16), (((1,), (1,)), ((), ())),
                           preferred_element_type=f32)


def _dot_tn(a, b):
    return lax.dot_general(a.astype(bf16), b.astype(bf16), (((0,), (0,)), ((), ())),
                           preferred_element_type=f32)


def _bf16_parts(x):
    p0 = x.astype(bf16).astype(f32)
    r0 = x - p0
    p1 = r0.astype(bf16).astype(f32)
    p2 = (r0 - p1).astype(bf16).astype(f32)
    return p0, p1, p2


def _cumsum_rows(x):
    c, n = x.shape
    if c == SUBLANES:
        sub = lax.broadcasted_iota(jnp.int32, x.shape, 0)
        for d in (1, 2, 4):
            x = x + jnp.where(sub >= d, pltpu.roll(x, d, 0), 0.0)
        return x
    y = _dot(_tri(c).astype(f32), jnp.concatenate(_bf16_parts(x), axis=1))
    return (y[:, :n] + y[:, n:2 * n]) + y[:, 2 * n:]


def _first_cols_as_rows(x):
    c = x.shape[0]
    cp = max(c, LANES)
    sel = (lax.broadcasted_iota(jnp.int32, (SUBLANES, LANES), 0)
           == lax.broadcasted_iota(jnp.int32, (SUBLANES, LANES), 1)).astype(f32)
    pad = [jnp.zeros((cp - c, LANES), f32)] if cp > c else []
    stacked = jnp.concatenate([piece for p in _bf16_parts(x) for piece in [p] + pad], axis=0)
    y = _dot_nt(sel, stacked)
    return (y[:, 0:c] + y[:, cp:cp + c]) + y[:, 2 * cp:2 * cp + c]


def _tri(c):
    r = lax.broadcasted_iota(jnp.int32, (c, c), 0)
    s = lax.broadcasted_iota(jnp.int32, (c, c), 1)
    return r >= s


def _stack_heads(x):
    lane = lax.broadcasted_iota(jnp.int32, (1, x.shape[1]), 1)
    blocks = [jnp.where((lane >= h * LANES) & (lane < (h + 1) * LANES), x, 0.0)
              for h in range(x.shape[1] // LANES)]
    return jnp.concatenate(blocks, axis=0)


def _split_heads(x):
    return jnp.concatenate([x[:, h * LANES:(h + 1) * LANES] for h in range(x.shape[1] // LANES)],
                           axis=0)


def _params(n_axes):
    return pltpu.CompilerParams(dimension_semantics=("arbitrary",) * n_axes,
                                vmem_limit_bytes=VMEM_LIMIT)


def _in_proj_kernel(x_ref, g_ref, w_ref, b_ref, o_ref, xn_ref):
    @pl.when(pl.program_id(1) == 0)
    def _():
        x = x_ref[...]
        ms = jnp.mean(x * x, axis=-1, keepdims=True)
        xn_ref[...] = (x * lax.rsqrt(ms + EPS) * g_ref[...]).astype(bf16)

    o_ref[...] = jnp.dot(xn_ref[...], w_ref[...], preferred_element_type=f32) + b_ref[...]


def _in_proj(x, g, w, b, layer, *, tm, tn):
    m, d = x.shape
    n = w.shape[-1]
    return pl.pallas_call(
        _in_proj_kernel,
        grid=(m // tm, n // tn),
        in_specs=[pl.BlockSpec((tm, d), lambda i, j: (i, 0)),
                  pl.BlockSpec((None, 1, d), lambda i, j: (layer, 0, 0)),
                  pl.BlockSpec((None, d, tn), lambda i, j: (layer, 0, j)),
                  pl.BlockSpec((None, 1, tn), lambda i, j: (layer, 0, j))],
        out_specs=pl.BlockSpec((tm, tn), lambda i, j: (i, j)),
        out_shape=jax.ShapeDtypeStruct((m, n), f32),
        scratch_shapes=[pltpu.VMEM((tm, d), bf16)],
        compiler_params=_params(2),
        name="in_proj",
    )(x, g, w, b)


class _Seq:
    def __init__(self, batch, seq_pad, seq_real, rows, has_state):
        self.batch = batch
        self.seq_pad = seq_pad
        self.seq_real = seq_real
        self.rows = rows
        self.has_state = has_state
        self.short = seq_pad <= rows
        if self.short:
            assert rows % seq_pad == 0 and seq_pad == SUBLANES
            self.seqs_per_tile = rows // seq_pad
            self.tiles_per_seq = 1
            self.gla_chunk = self.mlstm_chunk = seq_pad
        else:
            assert seq_pad % rows == 0 and seq_pad == seq_real
            self.seqs_per_tile = 1
            self.tiles_per_seq = seq_pad // rows
            self.gla_chunk = GLA_CHUNK
            self.mlstm_chunk = min(MLSTM_CHUNK, rows)
        self.n_tiles = batch * seq_pad // rows

    def state_spec(self, tail):
        nt = len(tail)
        tps = self.tiles_per_seq
        return pl.BlockSpec((self.seqs_per_tile,) + tuple(tail),
                            lambda i: (i // tps,) + (0,) * nt)

    def in_state_spec(self, tail, layer):
        nt = len(tail)
        return pl.BlockSpec((None, self.seqs_per_tile) + tuple(tail),
                            lambda i: (layer, i) + (0,) * nt)


def _conv_carry(seq, ucat_ref):
    if seq.short:
        return
    r = seq.rows
    i = pl.program_id(0)
    width = ucat_ref.shape[-1]

    @pl.when(i % seq.tiles_per_seq == 0)
    def _():
        ucat_ref[pl.ds(0, SUBLANES), :] = jnp.zeros((SUBLANES, width), f32)

    @pl.when(i % seq.tiles_per_seq != 0)
    def _():
        ucat_ref[pl.ds(0, SUBLANES), :] = ucat_ref[pl.ds(r, SUBLANES), :]


def _conv_store(seq, ucat_ref, u, state_ref, width, cols):
    ucat_ref[pl.ds(SUBLANES, seq.rows), cols] = u
    if seq.short:
        for s in range(seq.seqs_per_tile):
            lo = SUBLANES + s * seq.seq_pad - (width - 1)
            if seq.has_state:
                ucat_ref[pl.ds(lo, width - 1), cols] = state_ref[s, :, cols]
            else:
                ucat_ref[pl.ds(lo, width - 1), cols] = jnp.zeros((width - 1, u.shape[-1]), f32)


def _conv_piece(ucat_ref, w_ref, b_ref, r0, rows, cols, width):
    u = ucat_ref[pl.ds(r0, rows + SUBLANES), cols]
    s = u * w_ref[0:1, cols]
    for j in range(1, width - 1):
        s = u * w_ref[j:j + 1, cols] + _shift_rows(s)
    y = (b_ref[:, cols] + u * w_ref[width - 1:width, cols]) + _shift_rows(s)
    return y[SUBLANES:, :]


def _shift_rows(x):
    n_tiles = x.shape[0] // SUBLANES
    first = lax.broadcasted_iota(jnp.int32, (SUBLANES, x.shape[1]), 0) == 0
    rot = [pltpu.roll(x[t * SUBLANES:(t + 1) * SUBLANES, :], 1, 0) for t in range(n_tiles)]
    return jnp.concatenate([jnp.where(first, rot[t - 1], rot[t]) for t in range(n_tiles)], axis=0)


def _conv_state_out(seq, ucat_ref, out_ref, width):
    r = seq.rows
    i = pl.program_id(0)
    if seq.short:
        for s in range(seq.seqs_per_tile):
            lo = SUBLANES + s * seq.seq_pad + seq.seq_real - (width - 1)
            out_ref[s] = ucat_ref[pl.ds(lo, width - 1), :]
    else:
        @pl.when(i % seq.tiles_per_seq == seq.tiles_per_seq - 1)
        def _():
            out_ref[0] = ucat_ref[pl.ds(SUBLANES + r - (width - 1), width - 1), :]


def _gla_kernel(*refs, seq):
    if seq.has_state:
        (q_ref, k_ref, v_ref, g_ref, a_ref, wa_ref, ba_ref, gn_ref, s0_ref,
         o_ref, sout_ref, st_ref, lg_sc) = refs
    else:
        (q_ref, k_ref, v_ref, g_ref, a_ref, wa_ref, ba_ref, gn_ref,
         o_ref, sout_ref, st_ref, lg_sc) = refs
        s0_ref = None
    c = seq.gla_chunk
    i = pl.program_id(0)
    heads = range(GLA_HEADS)
    mask4 = ((lax.broadcasted_iota(jnp.int32, (GLA_HEADS * c, c), 0) & (c - 1))
             >= lax.broadcasted_iota(jnp.int32, (GLA_HEADS * c, c), 1))
    zpad = jnp.zeros((LANES - GLA_DK, GLA_DV), f32)

    for r0 in range(0, seq.rows, LANES):
        rows = pl.ds(r0, LANES)
        lg_sc[rows, :] = _log_sigmoid(_dot(a_ref[rows, :], wa_ref[...]) + ba_ref[...]) / GLA_TAU

    def chunk(rows, st):
        q = q_ref[rows, :] * (GLA_DK ** -0.5)
        k = k_ref[rows, :]
        v = v_ref[rows, :]
        g = g_ref[rows, :]
        lg = lg_sc[rows, :]
        if seq.seq_real < c:
            valid = lax.broadcasted_iota(jnp.int32, (c, 1), 0) < seq.seq_real
            lg = jnp.where(valid, lg, 0.0)
            k = jnp.where(valid, k, 0.0)
        bcum = _cumsum_rows(lg)
        blast = bcum[c - 1:c, :]
        q_d = q * jnp.exp(bcum)
        k_d = k * jnp.exp(-bcum)
        k_e = k * jnp.exp(blast - bcum)
        qd_st = _stack_heads(q_d).astype(bf16)
        a_st = jnp.where(mask4, _dot_nt(qd_st, k_d), 0.0)
        o_intra = _dot(a_st, v)
        o_inter = _dot_nt(qd_st, st)
        new_st = jnp.exp(blast) * st + _dot_tn(_split_heads(v), _stack_heads(k_e))
        for h in heads:
            sl = slice(h * LANES, (h + 1) * LANES)
            o_h = o_intra[h * c:(h + 1) * c, sl] + o_inter[h * c:(h + 1) * c, :]
            o_ref[rows, sl] = _head_rmsnorm(o_h, gn_ref[:, sl]) * _silu(g[:, sl])
        return new_st

    def state_out(idx, st):
        for h in heads:
            sout_ref[idx, h] = st[:, h * LANES:(h + 1) * LANES].T[0:GLA_DK, :]

    if seq.short:
        def body(ci, carry):
            rows = pl.ds(pl.multiple_of(ci * c, c), c)
            if seq.has_state:
                st = jnp.concatenate(
                    [jnp.concatenate([s0_ref[ci, h], zpad], axis=0).T for h in heads], axis=1)
            else:
                st = jnp.zeros((GLA_DV, GLA_HEADS * LANES), f32)
            state_out(ci, chunk(rows, st))
            return carry

        lax.fori_loop(0, seq.rows // c, body, 0, unroll=SHORT_UNROLL)
    else:
        @pl.when(i % seq.tiles_per_seq == 0)
        def _():
            st_ref[...] = jnp.zeros_like(st_ref)

        st = st_ref[...]
        for ci in range(seq.rows // c):
            st = chunk(pl.ds(ci * c, c), st)
        st_ref[...] = st

        @pl.when(i % seq.tiles_per_seq == seq.tiles_per_seq - 1)
        def _():
            state_out(0, st)


def _gla(proj, wa, ba, gn, s0, layer, seq):
    m = proj.shape[0]
    r = seq.rows

    def col(cb, w):
        return pl.BlockSpec((r, w), lambda i: (i, cb))

    in_specs = [col(COL_GQ // 512, 512), col(COL_GK // 512, 512), col(COL_GV // 512, 512),
                col(COL_GG // 512, 512), col(COL_GA // LANES, LANES),
                pl.BlockSpec((None, LANES, 512), lambda i: (layer, 0, 0)),
                pl.BlockSpec((None, 1, 512), lambda i: (layer, 0, 0)),
                pl.BlockSpec((None, 1, 512), lambda i: (layer, 0, 0))]
    args = [proj, proj, proj, proj, proj, wa, ba, gn]
    tail = (GLA_HEADS, GLA_DK, GLA_DV)
    if seq.has_state:
        in_specs.append(seq.in_state_spec(tail, layer))
        args.append(s0)
    return pl.pallas_call(
        functools.partial(_gla_kernel, seq=seq),
        grid=(seq.n_tiles,),
        in_specs=in_specs,
        out_specs=[pl.BlockSpec((r, 512), lambda i: (i, 0)), seq.state_spec(tail)],
        out_shape=[jax.ShapeDtypeStruct((m, 512), f32),
                   jax.ShapeDtypeStruct((seq.batch,) + tail, f32)],
        scratch_shapes=[pltpu.VMEM((GLA_DV, GLA_HEADS * LANES), f32), pltpu.VMEM((r, 512), f32)],
        compiler_params=_params(1),
        name="gla",
    )(*args)


def _mlstm_kernel(*refs, seq):
    n_in = 9
    (mu_ref, mo_ref, mif_ref, cw_ref, cb_ref, wq_ref, wk_ref, wv_ref, gn_ref) = refs[:n_in]
    if seq.has_state:
        c0_ref, n0_ref, m0_ref, conv0_ref = refs[n_in:n_in + 4]
        rest = refs[n_in + 4:]
    else:
        c0_ref = n0_ref = m0_ref = conv0_ref = None
        rest = refs[n_in:]
    (o_ref, cout_ref, nout_ref, mout_ref, convout_ref,
     ucat_ref, mcv_sc, q_sc, k_sc, v_sc, c_st, n_st, m_st) = rest
    c = seq.mlstm_chunk
    i = pl.program_id(0)
    heads = range(M_HEADS)
    mask = _tri(c)
    lane = lax.broadcasted_iota(jnp.int32, (c, LANES), 1)
    lane_h = lax.broadcasted_iota(jnp.int32, (1, M_HEADS), 1)

    mu = mu_ref[...]
    _conv_carry(seq, ucat_ref)
    _conv_store(seq, ucat_ref, mu, conv0_ref, CONV_W, slice(0, M_WIDTH))
    _conv_state_out(seq, ucat_ref, convout_ref, CONV_W)
    for r0 in range(0, seq.rows, CONV_ROWS):
        for c0 in range(0, M_WIDTH, CONV_COLS):
            cols = slice(c0, c0 + CONV_COLS)
            mcv_sc[pl.ds(r0, CONV_ROWS), cols] = _silu(
                _conv_piece(ucat_ref, cw_ref, cb_ref, r0, CONV_ROWS, cols, CONV_W)).astype(bf16)
    for h in heads:
        sl = slice(h * M_DH, (h + 1) * M_DH)
        q_sc[:, sl] = _dot(mcv_sc[:, sl], wq_ref[h])
        k_sc[:, sl] = _dot(mcv_sc[:, sl], wk_ref[h]) * (M_DH ** -0.5)
        v_sc[:, sl] = _dot(mu[:, sl], wv_ref[h])

    def chunk(rows, c_all, n_all, m_prevs):
        gi = mif_ref[rows, :]
        lf = _log_sigmoid(gi)
        if seq.seq_real < c:
            valid_c = lax.broadcasted_iota(jnp.int32, (c, 1), 0) < seq.seq_real
            valid_r = lax.broadcasted_iota(jnp.int32, (1, c), 1) < seq.seq_real
            lf = jnp.where(valid_c, lf, 0.0)
        fcum = _cumsum_rows(lf)
        xt = _first_cols_as_rows(jnp.where(lane < M_HEADS, gi, fcum))
        q = q_sc[rows, :]
        k = k_sc[rows, :]
        v = v_sc[rows, :]
        mo = mo_ref[rows, :]
        q_st = _stack_heads(q).astype(bf16)
        qk_st = _dot_nt(q_st, k)
        inter_st = _dot_nt(q_st, c_all)
        p_blocks, ws_blocks, per_head = [], [], []
        for h in heads:
            sl = slice(h * M_DH, (h + 1) * M_DH)
            f_col = fcum[:, M_HEADS + h:M_HEADS + h + 1]
            i_col = gi[:, h:h + 1]
            f_row = xt[M_HEADS + h:M_HEADS + h + 1, :]
            i_row = xt[h:h + 1, :]
            if seq.seq_real < c:
                i_col = jnp.where(valid_c, i_col, -jnp.inf)
                i_row = jnp.where(valid_r, i_row, -jnp.inf)
            dm = jnp.where(mask, f_col - f_row + i_row, -jnp.inf)
            b = f_col + m_prevs[h]
            mt = jnp.maximum(b, jnp.max(dm, axis=-1, keepdims=True))
            w_inter = jnp.exp(b - mt)
            p = jnp.exp(dm - mt) * qk_st[h * c:(h + 1) * c, :]
            den = (w_inter * jnp.sum(q[:, sl] * n_all[:, sl], axis=-1, keepdims=True)
                   + jnp.sum(p, axis=-1, keepdims=True))
            m_new = mt[c - 1:c, :]
            dec = jnp.exp(b[c - 1:c, :] - m_new)
            ws = jnp.exp(f_col[c - 1:c, :] - f_col + i_col - m_new)
            p_blocks.append(p)
            ws_blocks.append(ws)
            per_head.append((w_inter, den, mt, m_new, dec))
        pv_st = _dot(jnp.concatenate(p_blocks, axis=0), v)
        wsv_st = jnp.concatenate([ws_blocks[h] * v[:, h * M_DH:(h + 1) * M_DH] for h in heads],
                                 axis=0)
        dec_all = jnp.concatenate([jnp.broadcast_to(per_head[h][4], (1, M_DH)) for h in heads],
                                  axis=1)
        new_c = dec_all * c_all + _dot_tn(wsv_st, _stack_heads(k))
        new_n = dec_all * n_all + jnp.concatenate(
            [jnp.sum(ws_blocks[h] * k[:, h * M_DH:(h + 1) * M_DH], axis=0, keepdims=True)
             for h in heads], axis=1)
        for h in heads:
            sl = slice(h * M_DH, (h + 1) * M_DH)
            w_inter, den, mt, _, _ = per_head[h]
            num = w_inter * inter_st[h * c:(h + 1) * c, :] + pv_st[h * c:(h + 1) * c, sl]
            hh = num / jnp.maximum(jnp.abs(den), jnp.exp(-mt))
            o_ref[rows, sl] = _head_rmsnorm(hh, gn_ref[:, sl]) * jax.nn.sigmoid(mo[:, sl])
        return new_c, new_n, [per_head[h][3] for h in heads]

    def store_state(idx, c_all, n_all, m_prevs):
        m_row = jnp.zeros((1, M_HEADS), f32)
        for h in heads:
            sl = slice(h * M_DH, (h + 1) * M_DH)
            cout_ref[idx, h] = c_all[:, sl]
            nout_ref[idx, pl.ds(h, 1), :] = n_all[:, sl]
            m_row = jnp.where(lane_h == h, m_prevs[h], m_row)
        mout_ref[idx] = m_row

    if seq.short:
        def body(ci, carry):
            rows = pl.ds(pl.multiple_of(ci * c, c), c)
            if seq.has_state:
                n0 = n0_ref[ci]
                m0 = m0_ref[ci]
                state = (jnp.concatenate([c0_ref[ci, h] for h in heads], axis=1),
                         jnp.concatenate([n0[h:h + 1, :] for h in heads], axis=1),
                         [m0[:, h:h + 1] for h in heads])
            else:
                state = (jnp.zeros((M_DH, M_WIDTH), f32), jnp.zeros((1, M_WIDTH), f32),
                         [jnp.zeros((1, 1), f32)] * M_HEADS)
            store_state(ci, *chunk(rows, *state))
            return carry

        lax.fori_loop(0, seq.rows // c, body, 0, unroll=SHORT_UNROLL)
    else:
        @pl.when(i % seq.tiles_per_seq == 0)
        def _():
            c_st[...] = jnp.zeros_like(c_st)
            n_st[...] = jnp.zeros_like(n_st)
            m_st[...] = jnp.zeros_like(m_st)

        state = (c_st[...], n_st[0:1, :], [m_st[h:h + 1, 0:1] for h in heads])
        for ci in range(seq.rows // c):
            state = chunk(pl.ds(ci * c, c), *state)
        c_st[...] = state[0]
        n_st[0:1, :] = state[1]
        for h in heads:
            m_st[h:h + 1, :] = jnp.broadcast_to(state[2][h], (1, LANES))

        @pl.when(i % seq.tiles_per_seq == seq.tiles_per_seq - 1)
        def _():
            store_state(0, *state)


def _mlstm(proj, cw, cb, wq, wk, wv, gn, states, layer, seq):
    m = proj.shape[0]
    r = seq.rows

    def col(cb_, w):
        return pl.BlockSpec((r, w), lambda i: (i, cb_))

    def lw(shape):
        n = len(shape)
        return pl.BlockSpec((None,) + shape, lambda i: (layer,) + (0,) * n)

    in_specs = [col(COL_MU // 512, 512), col(COL_MO // 512, 512), col(COL_MIF // LANES, LANES),
                lw((CONV_W, M_WIDTH)), lw((1, M_WIDTH)),
                lw((M_HEADS, M_DH, M_DH)), lw((M_HEADS, M_DH, M_DH)), lw((M_HEADS, M_DH, M_DH)),
                lw((1, M_WIDTH))]
    args = [proj, proj, proj, cw, cb, wq, wk, wv, gn]
    tails = [(M_HEADS, M_DH, M_DH), (M_HEADS, M_DH), (1, M_HEADS), (CONV_W - 1, M_WIDTH)]
    if seq.has_state:
        in_specs += [seq.in_state_spec(t, layer) for t in tails]
        args += list(states)
    return pl.pallas_call(
        functools.partial(_mlstm_kernel, seq=seq),
        grid=(seq.n_tiles,),
        in_specs=in_specs,
        out_specs=[pl.BlockSpec((r, M_WIDTH), lambda i: (i, 0))]
        + [seq.state_spec(t) for t in tails],
        out_shape=[jax.ShapeDtypeStruct((m, M_WIDTH), f32)]
        + [jax.ShapeDtypeStruct((seq.batch,) + t, f32) for t in tails],
        scratch_shapes=[pltpu.VMEM((r + SUBLANES, M_WIDTH), f32), pltpu.VMEM((r, M_WIDTH), bf16),
                        pltpu.VMEM((r, M_WIDTH), f32), pltpu.VMEM((r, M_WIDTH), f32),
                        pltpu.VMEM((r, M_WIDTH), f32),
                        pltpu.VMEM((M_DH, M_WIDTH), f32),
                        pltpu.VMEM((SUBLANES, M_WIDTH), f32), pltpu.VMEM((SUBLANES, LANES), f32)],
        compiler_params=_params(1),
        name="mlstm",
    )(*args)


def _rglru_kernel(*refs, seq):
    n_in = 9
    (rx_ref, ry_ref, cw_ref, cb_ref, wr_ref, br_ref, wi_ref, bi_ref, lam_ref) = refs[:n_in]
    if seq.has_state:
        h0_ref, conv0_ref = refs[n_in:n_in + 2]
        rest = refs[n_in + 2:]
    else:
        h0_ref = conv0_ref = None
        rest = refs[n_in:]
    o_ref, hout_ref, convout_ref, ucat_ref, xc_sc, a_sc, b_sc, h_st = rest
    r = seq.rows
    i = pl.program_id(0)

    _conv_carry(seq, ucat_ref)
    _conv_store(seq, ucat_ref, rx_ref[...], conv0_ref, CONV_W, slice(0, R_WIDTH))
    _conv_state_out(seq, ucat_ref, convout_ref, CONV_W)
    pieces = [(r0, slice(c0, c0 + CONV_COLS))
              for r0 in range(0, r, CONV_ROWS) for c0 in range(0, R_WIDTH, CONV_COLS)]
    for r0, cols in pieces:
        xc_sc[pl.ds(r0, CONV_ROWS), cols] = _conv_piece(ucat_ref, cw_ref, cb_ref, r0, CONV_ROWS,
                                                        cols, CONV_W)
    xc_bf = xc_sc[...].astype(bf16)
    a_sc[...] = jnp.dot(xc_bf, wr_ref[...], preferred_element_type=f32) + br_ref[...]
    b_sc[...] = jnp.dot(xc_bf, wi_ref[...], preferred_element_type=f32) + bi_ref[...]
    log_lam = _log_sigmoid(lam_ref[...])
    for r0, cols in pieces:
        rows = pl.ds(r0, CONV_ROWS)
        log_a = R_C * jax.nn.sigmoid(a_sc[rows, cols]) * log_lam[:, cols]
        a = jnp.exp(log_a)
        gated_x = jax.nn.sigmoid(b_sc[rows, cols]) * xc_sc[rows, cols]
        a_sc[rows, cols] = a
        b_sc[rows, cols] = jnp.sqrt(-jnp.tanh(log_a) * (a * a + 1.0)) * gated_x

    if not seq.short:
        @pl.when(i % seq.tiles_per_seq == 0)
        def _():
            h_st[...] = jnp.zeros_like(h_st)

    sub = lax.broadcasted_iota(jnp.int32, (SUBLANES, R_WIDTH), 0)

    def body(gi, h_prev):
        rows = pl.ds(pl.multiple_of(gi * SUBLANES, SUBLANES), SUBLANES)
        ag = a_sc[rows, :]
        bg = b_sc[rows, :]
        for d in (1, 2, 4):
            keep = sub >= d
            bg = jnp.where(keep, ag * pltpu.roll(bg, d, 0) + bg, bg)
            ag = jnp.where(keep, ag * pltpu.roll(ag, d, 0), ag)
        if seq.short:
            if seq.has_state:
                h_in = h0_ref[gi]
            else:
                h_in = jnp.zeros((1, R_WIDTH), f32)
        else:
            h_in = h_prev
        hg = bg + ag * h_in
        o_ref[rows, :] = hg * _gelu_tanh(ry_ref[rows, :])
        if seq.short:
            hout_ref[gi] = hg[seq.seq_real - 1:seq.seq_real, :]
            return h_prev
        return hg[SUBLANES - 1:SUBLANES, :]

    h_last = lax.fori_loop(0, r // SUBLANES, body, h_st[0:1, :])

    if not seq.short:
        h_st[0:1, :] = h_last

        @pl.when(i % seq.tiles_per_seq == seq.tiles_per_seq - 1)
        def _():
            hout_ref[0] = h_last


def _rglru(proj, cw, cb, wr, br, wi, bi, lam, states, layer, seq):
    m = proj.shape[0]
    r = seq.rows

    def col(cb_, w):
        return pl.BlockSpec((r, w), lambda i: (i, cb_))

    def lw(shape):
        n = len(shape)
        return pl.BlockSpec((None,) + shape, lambda i: (layer,) + (0,) * n)

    in_specs = [col(COL_RX // 512, 512), col(COL_RY // 512, 512),
                lw((CONV_W, R_WIDTH)), lw((1, R_WIDTH)),
                lw((R_WIDTH, R_WIDTH)), lw((1, R_WIDTH)),
                lw((R_WIDTH, R_WIDTH)), lw((1, R_WIDTH)), lw((1, R_WIDTH))]
    args = [proj, proj, cw, cb, wr, br, wi, bi, lam]
    tails = [(1, R_WIDTH), (CONV_W - 1, R_WIDTH)]
    if seq.has_state:
        in_specs += [seq.in_state_spec(t, layer) for t in tails]
        args += list(states)
    return pl.pallas_call(
        functools.partial(_rglru_kernel, seq=seq),
        grid=(seq.n_tiles,),
        in_specs=in_specs,
        out_specs=[pl.BlockSpec((r, R_WIDTH), lambda i: (i, 0))]
        + [seq.state_spec(t) for t in tails],
        out_shape=[jax.ShapeDtypeStruct((m, R_WIDTH), f32)]
        + [jax.ShapeDtypeStruct((seq.batch,) + t, f32) for t in tails],
        scratch_shapes=[pltpu.VMEM((r + SUBLANES, R_WIDTH), f32), pltpu.VMEM((r, R_WIDTH), f32),
                        pltpu.VMEM((r, R_WIDTH), f32), pltpu.VMEM((r, R_WIDTH), f32),
                        pltpu.VMEM((SUBLANES, R_WIDTH), f32)],
        compiler_params=_params(1),
        name="rglru",
    )(*args)


def _merge_ffn_kernel(*refs, seq, final_norm):
    n_in = 13
    (gates_ref, og_ref, om_ref, or_ref, x_ref, wb_ref, wo_ref,
     g_ref, wu_ref, cw_ref, cb_ref, wd_ref, gf_ref) = refs[:n_in]
    if seq.has_state:
        conv0_ref = refs[n_in]
        rest = refs[n_in + 1:]
    else:
        conv0_ref = None
        rest = refs[n_in:]
    o_ref, convout_ref, ucat_ref, act_sc, mix_sc = rest

    branches = (og_ref, om_ref, or_ref)
    o_bf = [b_ref[...].astype(bf16) for b_ref in branches]
    for c0 in range(0, D_MODEL, CONV_COLS):
        mix = None
        for j in range(len(branches)):
            gate = jax.nn.sigmoid(gates_ref[:, j * D_MODEL + c0:j * D_MODEL + c0 + CONV_COLS])
            term = gate * jnp.dot(o_bf[j], wb_ref[j * 512:(j + 1) * 512, c0:c0 + CONV_COLS],
                                  preferred_element_type=f32)
            mix = term if mix is None else mix + term
        mix_sc[:, c0:c0 + CONV_COLS] = mix.astype(bf16)
    x = x_ref[...] + jnp.dot(mix_sc[...], wo_ref[...], preferred_element_type=f32)

    xn = (x * lax.rsqrt(jnp.mean(x * x, axis=-1, keepdims=True) + EPS) * g_ref[...]).astype(bf16)
    _conv_carry(seq, ucat_ref)
    y = x
    for g0 in range(0, D_FF, FFN_GROUP):
        g1 = min(g0 + FFN_GROUP, D_FF)
        for c0 in range(g0, g1, CONV_COLS):
            halves = [slice(base + c0, base + c0 + CONV_COLS) for base in (0, D_FF)]
            for cols in halves:
                u = jnp.dot(xn, wu_ref[:, cols], preferred_element_type=f32)
                _conv_store(seq, ucat_ref, u, conv0_ref, FFN_CONV_W, cols)
            for r0 in range(0, seq.rows, CONV_ROWS):
                for p0 in range(c0, c0 + CONV_COLS, LANES):
                    u_a, u_b = (_conv_piece(ucat_ref, cw_ref, cb_ref, r0, CONV_ROWS,
                                            slice(base + p0, base + p0 + LANES), FFN_CONV_W)
                                for base in (0, D_FF))
                    act_sc[pl.ds(r0, CONV_ROWS), p0:p0 + LANES] = (
                        _gelu_tanh(u_a) * u_b).astype(bf16)
        y = y + jnp.dot(act_sc[:, g0:g1], wd_ref[g0:g1, :], preferred_element_type=f32)
    _conv_state_out(seq, ucat_ref, convout_ref, FFN_CONV_W)
    if final_norm:
        y = y * lax.rsqrt(jnp.mean(y * y, axis=-1, keepdims=True) + EPS) * gf_ref[...]
    o_ref[...] = y


def _merge_ffn(proj, o_g, o_m, o_r, x, wts, conv0, layer, seq, final_norm):
    m = x.shape[0]
    r = seq.rows
    row = lambda w: pl.BlockSpec((r, w), lambda i: (i, 0))

    def lw(shape, **kw):
        n = len(shape)
        return pl.BlockSpec((None,) + shape, lambda i: (layer,) + (0,) * n, **kw)

    once = dict(pipeline_mode=pl.Buffered(1))
    in_specs = [row(3 * D_MODEL), row(512), row(512), row(512), row(D_MODEL),
                lw((1536, D_MODEL), **once), lw((D_MODEL, D_MODEL), **once),
                lw((1, D_MODEL)), lw((D_MODEL, 2 * D_FF), **once),
                lw((FFN_CONV_W, 2 * D_FF)), lw((1, 2 * D_FF)), lw((D_FF, D_MODEL), **once),
                pl.BlockSpec((1, D_MODEL), lambda i: (0, 0))]
    args = [proj, o_g, o_m, o_r, x, wts['w_branch'], wts['w_out'], wts['norm_ffn_g'], wts['w_up'],
            wts['ffn_conv_w'], wts['ffn_conv_b'], wts['w_down'], wts['norm_f_g']]
    tail = (FFN_CONV_W - 1, 2 * D_FF)
    if seq.has_state:
        in_specs.append(seq.in_state_spec(tail, layer))
        args.append(conv0)
    return pl.pallas_call(
        functools.partial(_merge_ffn_kernel, seq=seq, final_norm=final_norm),
        grid=(seq.n_tiles,),
        in_specs=in_specs,
        out_specs=[row(D_MODEL), seq.state_spec(tail)],
        out_shape=[jax.ShapeDtypeStruct((m, D_MODEL), f32),
                   jax.ShapeDtypeStruct((seq.batch,) + tail, f32)],
        scratch_shapes=[pltpu.VMEM((r + SUBLANES, 2 * D_FF), f32), pltpu.VMEM((r, D_FF), bf16),
                        pltpu.VMEM((r, D_MODEL), bf16)],
        compiler_params=_params(1),
        name="merge_ffn",
    )(*args)


def _head_pad(a):
    lead = a.shape[:-1]
    a = a.reshape(lead + (GLA_HEADS, GLA_DK))
    a = jnp.pad(a, [(0, 0)] * len(lead) + [(0, 0), (0, LANES - GLA_DK)])
    return a.reshape(lead + (GLA_HEADS * LANES,))


def _lane_pad(a, width):
    return jnp.pad(a, [(0, 0)] * (a.ndim - 1) + [(0, width - a.shape[-1])])


def _permute_in_cols(a):
    sizes = (256, 256, 512, 512, GLA_RANK, 512, M_HEADS, M_HEADS, 512, 512, 512, 3 * D_MODEL)
    parts = []
    off = 0
    for s in sizes:
        parts.append(a[..., off:off + s])
        off += s
    gq, gk, gv, gg, ga, mu, mi, mf, mo, rx, ry, gates = parts
    out = jnp.concatenate(
        [gates, _head_pad(gq), _head_pad(gk), gv, gg, mu, mo, rx, ry,
         _lane_pad(ga, LANES), _lane_pad(jnp.concatenate([mi, mf], axis=-1), LANES)], axis=-1)
    return _lane_pad(out, D_IN_PAD)


def _block_diag(w):
    d, nb, bs, _ = w.shape
    eye = jnp.eye(nb, dtype=w.dtype)
    return jnp.einsum('lnde,nm->lndme', w, eye).reshape(d, nb * bs, nb * bs)


def _run_group(x, seq_mix, seq_ffn, states, wts, tm):
    per_layer = []
    for l in range(DEPTH):
        proj = _in_proj(x, wts['norm_mix_g'], wts['w_in'], wts['b_in'], l, tm=tm, tn=2560)
        st = states
        o_g, s_new = _gla(proj, wts['w_gla_a2'], wts['b_gla_a2'], wts['gla_norm_g'],
                          st[0] if st else None, l, seq_mix)
        o_m, c_new, n_new, m_new, mconv_new = _mlstm(
            proj, wts['mlstm_conv_w'], wts['mlstm_conv_b'], wts['w_mlstm_q'], wts['w_mlstm_k'],
            wts['w_mlstm_v'], wts['mlstm_norm_g'], st[1:5] if st else None, l, seq_mix)
        o_r, h_new, rconv_new = _rglru(
            proj, wts['rglru_conv_w'], wts['rglru_conv_b'], wts['w_rglru_r'], wts['b_rglru_r'],
            wts['w_rglru_i'], wts['b_rglru_i'], wts['rglru_lambda'], st[5:7] if st else None,
            l, seq_mix)
        x, fconv_new = _merge_ffn(proj, o_g, o_m, o_r, x, wts, st[7] if st else None, l, seq_ffn,
                                  final_norm=(l == DEPTH - 1))
        per_layer.append([s_new, c_new, n_new, m_new, mconv_new, h_new, rconv_new, fconv_new])
    new_states = [jnp.stack([st[j] for st in per_layer]) for j in range(8)]
    return x, new_states


def kernel(x_prompt, x_sample, state_gla_S, state_mlstm_C, state_mlstm_n, state_mlstm_m,
           state_mlstm_conv, state_rglru_h, state_rglru_conv, state_ffn_conv,
           norm_mix_g, w_in, b_in, w_gla_a2, b_gla_a2, gla_norm_g,
           mlstm_conv_w, mlstm_conv_b, w_mlstm_q, w_mlstm_k, w_mlstm_v, mlstm_norm_g,
           rglru_conv_w, rglru_conv_b, w_rglru_r, b_rglru_r, w_rglru_i, b_rglru_i, rglru_lambda,
           w_branch, w_out, norm_ffn_g, w_up, ffn_conv_w, ffn_conv_b, w_down, norm_f_g):
    row = lambda a: a[:, None, :]
    wts = {
        'norm_mix_g': row(norm_mix_g),
        'w_in': _permute_in_cols(w_in).astype(bf16),
        'b_in': row(_permute_in_cols(b_in)),
        'w_gla_a2': _head_pad(jnp.pad(w_gla_a2, ((0, 0), (0, LANES - GLA_RANK), (0, 0)))).astype(bf16),
        'b_gla_a2': row(_head_pad(b_gla_a2)),
        'gla_norm_g': row(gla_norm_g),
        'mlstm_conv_w': mlstm_conv_w, 'mlstm_conv_b': row(mlstm_conv_b),
        'w_mlstm_q': w_mlstm_q.astype(bf16), 'w_mlstm_k': w_mlstm_k.astype(bf16),
        'w_mlstm_v': w_mlstm_v.astype(bf16), 'mlstm_norm_g': row(mlstm_norm_g),
        'rglru_conv_w': rglru_conv_w, 'rglru_conv_b': row(rglru_conv_b),
        'w_rglru_r': _block_diag(w_rglru_r).astype(bf16), 'b_rglru_r': row(b_rglru_r),
        'w_rglru_i': _block_diag(w_rglru_i).astype(bf16), 'b_rglru_i': row(b_rglru_i),
        'rglru_lambda': row(rglru_lambda),
        'w_branch': w_branch.astype(bf16), 'w_out': w_out.astype(bf16),
        'norm_ffn_g': row(norm_ffn_g), 'w_up': w_up.astype(bf16),
        'ffn_conv_w': ffn_conv_w, 'ffn_conv_b': row(ffn_conv_b),
        'w_down': w_down.astype(bf16), 'norm_f_g': norm_f_g[None, :],
    }

    bp, lp, _ = x_prompt.shape
    seq_p_mix = _Seq(bp, lp, lp, 512, has_state=False)
    seq_p_ffn = _Seq(bp, lp, lp, 256, has_state=False)
    y_p, p_states = _run_group(x_prompt.reshape(bp * lp, D_MODEL), seq_p_mix, seq_p_ffn,
                               None, wts, tm=1024)
    y_prompt = y_p.reshape(bp, lp, D_MODEL)

    bs, ls, _ = x_sample.shape
    xs = jnp.pad(x_sample, ((0, 0), (0, SUBLANES - ls), (0, 0))).reshape(bs * SUBLANES, D_MODEL)
    seq_s = _Seq(bs, SUBLANES, ls, 128, has_state=True)
    s_in = [state_gla_S, state_mlstm_C, state_mlstm_n, state_mlstm_m[:, :, None, :],
            state_mlstm_conv, state_rglru_h[:, :, None, :], state_rglru_conv, state_ffn_conv]
    y_s, s_states = _run_group(xs, seq_s, seq_s, s_in, wts, tm=1024)
    y_sample = y_s.reshape(bs, SUBLANES, D_MODEL)[:, :ls]

    def unpack(st):
        s, c, n, m, mconv, h, rconv, fconv = st
        return (s, c, n, m[:, :, 0, :], mconv, h[:, :, 0, :], rconv, fconv)

    return (y_prompt, y_sample) + unpack(p_states) + unpack(s_states)
```

```python
import functools

import jax
import jax.numpy as jnp
from jax import lax
from jax.experimental import pallas as pl
from jax.experimental.pallas import tpu as pltpu

f32 = jnp.float32
bf16 = jnp.bfloat16

D_MODEL = 1024
DEPTH = 4
GLA_HEADS = 4
GLA_DK = 64
GLA_DV = 128
GLA_RANK = 16
GLA_TAU = 16.0
M_HEADS = 4
M_DH = 128
M_WIDTH = 512
CONV_W = 4
R_WIDTH = 512
R_BLOCKS = 8
R_BS = 64
R_C = 8.0
D_FF = 2816
FFN_CONV_W = 3
EPS = 1e-6

LANES = 128
SUBLANES = 8
GLA_CHUNK = 64
MLSTM_CHUNK = 256
SHORT_UNROLL = 4
CONV_ROWS = 64
CONV_COLS = 256
FFN_GROUP = 1024
VMEM_LIMIT = 52 * 1024 * 1024

COL_GATES = 0
COL_GQ = 3072
COL_GK = 3584
COL_GV = 4096
COL_GG = 4608
COL_MU = 5120
COL_MO = 5632
COL_RX = 6144
COL_RY = 6656
COL_GA = 7168
COL_MIF = 7296
D_IN_PAD = 7680


def _log_sigmoid(x):
    return jnp.minimum(x, 0.0) - jnp.log(1.0 + jnp.exp(-jnp.abs(x)))


def _silu(x):
    return x * jax.nn.sigmoid(x)


def _gelu_tanh(x):
    hx = 0.5 * x
    return hx + hx * jnp.tanh(x * (0.7978845608028654 + (0.7978845608028654 * 0.044715) * (x * x)))


def _head_rmsnorm(o, g):
    return o * lax.rsqrt(jnp.mean(o * o, axis=-1, keepdims=True) + EPS) * g


def _dot(a, b):
    return jnp.dot(a.astype(bf16), b.astype(bf16), preferred_element_type=f32)


def _dot_nt(a, b):
    return lax.dot_general(a.astype(bf16), b.astype(bf16), (((1,), (1,)), ((), ())),
                           preferred_element_type=f32)


def _dot_tn(a, b):
    return lax.dot_general(a.astype(bf16), b.astype(bf16), (((0,), (0,)), ((), ())),
                           preferred_element_type=f32)


def _bf16_parts(x):
    p0 = x.astype(bf16).astype(f32)
    r0 = x - p0
    p1 = r0.astype(bf16).astype(f32)
    p2 = (r0 - p1).astype(bf16).astype(f32)
    return p0, p1, p2


def _cumsum_rows(x):
    c, n = x.shape
    if c == SUBLANES:
        sub = lax.broadcasted_iota(jnp.int32, x.shape, 0)
        for d in (1, 2, 4):
            x = x + jnp.where(sub >= d, pltpu.roll(x, d, 0), 0.0)
        return x
    y = _dot(_tri(c).astype(f32), jnp.concatenate(_bf16_parts(x), axis=1))
    return (y[:, :n] + y[:, n:2 * n]) + y[:, 2 * n:]


def _first_cols_as_rows(x):
    c = x.shape[0]
    cp = max(c, LANES)
    sel = (lax.broadcasted_iota(jnp.int32, (SUBLANES, LANES), 0)
           == lax.broadcasted_iota(jnp.int32, (SUBLANES, LANES), 1)).astype(f32)
    pad = [jnp.zeros((cp - c, LANES), f32)] if cp > c else []
    stacked = jnp.concatenate([piece for p in _bf16_parts(x) for piece in [p] + pad], axis=0)
    y = _dot_nt(sel, stacked)
    return (y[:, 0:c] + y[:, cp:cp + c]) + y[:, 2 * cp:2 * cp + c]


def _tri(c):
    r = lax.broadcasted_iota(jnp.int32, (c, c), 0)
    s = lax.broadcasted_iota(jnp.int32, (c, c), 1)
    return r >= s


def _stack_heads(x):
    lane = lax.broadcasted_iota(jnp.int32, (1, x.shape[1]), 1)
    blocks = [jnp.where((lane >= h * LANES) & (lane < (h + 1) * LANES), x, 0.0)
              for h in range(x.shape[1] // LANES)]
    return jnp.concatenate(blocks, axis=0)


def _split_heads(x):
    return jnp.concatenate([x[:, h * LANES:(h + 1) * LANES] for h in range(x.shape[1] // LANES)],
                           axis=0)


def _params(n_axes):
    return pltpu.CompilerParams(dimension_semantics=("arbitrary",) * n_axes,
                                vmem_limit_bytes=VMEM_LIMIT)


def _in_proj_kernel(x_ref, g_ref, w_ref, b_ref, o_ref, xn_ref):
    @pl.when(pl.program_id(1) == 0)
    def _():
        x = x_ref[...]
        ms = jnp.mean(x * x, axis=-1, keepdims=True)
        xn_ref[...] = (x * lax.rsqrt(ms + EPS) * g_ref[...]).astype(bf16)

    o_ref[...] = jnp.dot(xn_ref[...], w_ref[...], preferred_element_type=f32) + b_ref[...]


def _in_proj(x, g, w, b, layer, *, tm, tn):
    m, d = x.shape
    n = w.shape[-1]
    return pl.pallas_call(
        _in_proj_kernel,
        grid=(m // tm, n // tn),
        in_specs=[pl.BlockSpec((tm, d), lambda i, j: (i, 0)),
                  pl.BlockSpec((None, 1, d), lambda i, j: (layer, 0, 0)),
                  pl.BlockSpec((None, d, tn), lambda i, j: (layer, 0, j)),
                  pl.BlockSpec((None, 1, tn), lambda i, j: (layer, 0, j))],
        out_specs=pl.BlockSpec((tm, tn), lambda i, j: (i, j)),
        out_shape=jax.ShapeDtypeStruct((m, n), f32),
        scratch_shapes=[pltpu.VMEM((tm, d), bf16)],
        compiler_params=_params(2),
        name="in_proj",
    )(x, g, w, b)


class _Seq:
    def __init__(self, batch, seq_pad, seq_real, rows, has_state, par=1):
        assert batch % par == 0
        self.par = par
        self.batch = batch
        self.seq_pad = seq_pad
        self.seq_real = seq_real
        self.rows = rows
        self.has_state = has_state
        self.short = seq_pad <= rows
        if self.short:
            assert rows % seq_pad == 0 and seq_pad == SUBLANES
            self.seqs_per_tile = rows // seq_pad
            self.tiles_per_seq = 1
            self.gla_chunk = self.mlstm_chunk = seq_pad
        else:
            assert seq_pad % rows == 0 and seq_pad == seq_real
            self.seqs_per_tile = 1
            self.tiles_per_seq = seq_pad // rows
            self.gla_chunk = GLA_CHUNK
            self.mlstm_chunk = min(MLSTM_CHUNK, rows)
        self.n_tiles = batch * seq_pad // rows // par

    def state_spec(self, tail):
        nt = len(tail)
        tps = self.tiles_per_seq
        return pl.BlockSpec((self.seqs_per_tile,) + tuple(tail),
                            lambda i: (i // tps,) + (0,) * nt)

    def par_state_spec(self, tail):
        nt = len(tail)
        tps = self.tiles_per_seq
        return pl.BlockSpec((self.par, self.seqs_per_tile) + tuple(tail),
                            lambda i: (0, i // tps) + (0,) * nt)

    def par_rows_spec(self, width, col_block):
        return pl.BlockSpec((self.par, self.rows, width), lambda i: (0, i, col_block))

    def in_state_spec(self, tail, layer):
        nt = len(tail)
        return pl.BlockSpec((None, self.seqs_per_tile) + tuple(tail),
                            lambda i: (layer, i) + (0,) * nt)


def _conv_carry(seq, ucat_ref):
    if seq.short:
        return
    r = seq.rows
    i = pl.program_id(0)
    width = ucat_ref.shape[-1]

    @pl.when(i % seq.tiles_per_seq == 0)
    def _():
        ucat_ref[pl.ds(0, SUBLANES), :] = jnp.zeros((SUBLANES, width), f32)

    @pl.when(i % seq.tiles_per_seq != 0)
    def _():
        ucat_ref[pl.ds(0, SUBLANES), :] = ucat_ref[pl.ds(r, SUBLANES), :]


def _conv_store(seq, ucat_ref, u, state_ref, width, cols):
    ucat_ref[pl.ds(SUBLANES, seq.rows), cols] = u
    if seq.short:
        for s in range(seq.seqs_per_tile):
            lo = SUBLANES + s * seq.seq_pad - (width - 1)
            if seq.has_state:
                ucat_ref[pl.ds(lo, width - 1), cols] = state_ref[s, :, cols]
            else:
                ucat_ref[pl.ds(lo, width - 1), cols] = jnp.zeros((width - 1, u.shape[-1]), f32)


def _conv_piece(ucat_ref, w_ref, b_ref, r0, rows, cols, width):
    u = ucat_ref[pl.ds(r0, rows + SUBLANES), cols]
    s = u * w_ref[0:1, cols]
    for j in range(1, width - 1):
        s = u * w_ref[j:j + 1, cols] + _shift_rows(s)
    y = (b_ref[:, cols] + u * w_ref[width - 1:width, cols]) + _shift_rows(s)
    return y[SUBLANES:, :]


def _shift_rows(x):
    n_tiles = x.shape[0] // SUBLANES
    first = lax.broadcasted_iota(jnp.int32, (SUBLANES, x.shape[1]), 0) == 0
    rot = [pltpu.roll(x[t * SUBLANES:(t + 1) * SUBLANES, :], 1, 0) for t in range(n_tiles)]
    return jnp.concatenate([jnp.where(first, rot[t - 1], rot[t]) for t in range(n_tiles)], axis=0)


def _conv_state_out(seq, ucat_ref, out_ref, width):
    r = seq.rows
    i = pl.program_id(0)
    if seq.short:
        for s in range(seq.seqs_per_tile):
            lo = SUBLANES + s * seq.seq_pad + seq.seq_real - (width - 1)
            out_ref[s] = ucat_ref[pl.ds(lo, width - 1), :]
    else:
        @pl.when(i % seq.tiles_per_seq == seq.tiles_per_seq - 1)
        def _():
            out_ref[0] = ucat_ref[pl.ds(SUBLANES + r - (width - 1), width - 1), :]


def _gla_kernel(*refs, seq):
    if seq.has_state:
        (q_ref, k_ref, v_ref, g_ref, a_ref, wa_ref, ba_ref, gn_ref, s0_ref,
         o_ref, sout_ref, st_ref, lg_sc) = refs
    else:
        (q_ref, k_ref, v_ref, g_ref, a_ref, wa_ref, ba_ref, gn_ref,
         o_ref, sout_ref, st_ref, lg_sc) = refs
        s0_ref = None
    c = seq.gla_chunk
    i = pl.program_id(0)
    heads = range(GLA_HEADS)
    groups = range(seq.par)
    mask4 = ((lax.broadcasted_iota(jnp.int32, (GLA_HEADS * c, c), 0) & (c - 1))
             >= lax.broadcasted_iota(jnp.int32, (GLA_HEADS * c, c), 1))
    zpad = jnp.zeros((LANES - GLA_DK, GLA_DV), f32)

    for p in groups:
        for r0 in range(0, seq.rows, LANES):
            rows = pl.ds(r0, LANES)
            lg_sc[p, rows, :] = _log_sigmoid(
                _dot(a_ref[p, rows, :], wa_ref[...]) + ba_ref[...]) / GLA_TAU

    def chunk(p, rows, st):
        q = q_ref[p, rows, :] * (GLA_DK ** -0.5)
        k = k_ref[p, rows, :]
        v = v_ref[p, rows, :]
        g = g_ref[p, rows, :]
        lg = lg_sc[p, rows, :]
        if seq.seq_real < c:
            valid = lax.broadcasted_iota(jnp.int32, (c, 1), 0) < seq.seq_real
            lg = jnp.where(valid, lg, 0.0)
            k = jnp.where(valid, k, 0.0)
        bcum = _cumsum_rows(lg)
        blast = bcum[c - 1:c, :]
        q_d = q * jnp.exp(bcum)
        k_d = k * jnp.exp(-bcum)
        k_e = k * jnp.exp(blast - bcum)
        qd_st = _stack_heads(q_d).astype(bf16)
        a_st = jnp.where(mask4, _dot_nt(qd_st, k_d), 0.0)
        o_intra = _dot(a_st, v)
        o_inter = _dot_nt(qd_st, st)
        new_st = jnp.exp(blast) * st + _dot_tn(_split_heads(v), _stack_heads(k_e))
        for h in heads:
            sl = slice(h * LANES, (h + 1) * LANES)
            o_h = o_intra[h * c:(h + 1) * c, sl] + o_inter[h * c:(h + 1) * c, :]
            o_ref[p, rows, sl] = _head_rmsnorm(o_h, gn_ref[:, sl]) * _silu(g[:, sl])
        return new_st

    def state_out(p, idx, st):
        for h in heads:
            sout_ref[p, idx, h] = st[:, h * LANES:(h + 1) * LANES].T[0:GLA_DK, :]

    if seq.short:
        def body(ci, carry):
            rows = pl.ds(pl.multiple_of(ci * c, c), c)
            if seq.has_state:
                st = jnp.concatenate(
                    [jnp.concatenate([s0_ref[ci, h], zpad], axis=0).T for h in heads], axis=1)
            else:
                st = jnp.zeros((GLA_DV, GLA_HEADS * LANES), f32)
            state_out(0, ci, chunk(0, rows, st))
            return carry

        lax.fori_loop(0, seq.rows // c, body, 0, unroll=SHORT_UNROLL)
    else:
        @pl.when(i % seq.tiles_per_seq == 0)
        def _():
            st_ref[...] = jnp.zeros_like(st_ref)

        st = [st_ref[p] for p in groups]
        for ci in range(seq.rows // c):
            for p in groups:
                st[p] = chunk(p, pl.ds(ci * c, c), st[p])
        for p in groups:
            st_ref[p] = st[p]

        @pl.when(i % seq.tiles_per_seq == seq.tiles_per_seq - 1)
        def _():
            for p in groups:
                state_out(p, 0, st[p])


def _gla(proj, wa, ba, gn, s0, layer, seq):
    m, n = proj.shape
    r = seq.rows
    proj = proj.reshape(seq.par, m // seq.par, n)
    in_specs = [seq.par_rows_spec(512, COL_GQ // 512), seq.par_rows_spec(512, COL_GK // 512),
                seq.par_rows_spec(512, COL_GV // 512), seq.par_rows_spec(512, COL_GG // 512),
                seq.par_rows_spec(LANES, COL_GA // LANES),
                pl.BlockSpec((None, LANES, 512), lambda i: (layer, 0, 0)),
                pl.BlockSpec((None, 1, 512), lambda i: (layer, 0, 0)),
                pl.BlockSpec((None, 1, 512), lambda i: (layer, 0, 0))]
    args = [proj, proj, proj, proj, proj, wa, ba, gn]
    tail = (GLA_HEADS, GLA_DK, GLA_DV)
    if seq.has_state:
        in_specs.append(seq.in_state_spec(tail, layer))
        args.append(s0)
    o, s_new = pl.pallas_call(
        functools.partial(_gla_kernel, seq=seq),
        grid=(seq.n_tiles,),
        in_specs=in_specs,
        out_specs=[seq.par_rows_spec(512, 0), seq.par_state_spec(tail)],
        out_shape=[jax.ShapeDtypeStruct((seq.par, m // seq.par, 512), f32),
                   jax.ShapeDtypeStruct((seq.par, seq.batch // seq.par) + tail, f32)],
        scratch_shapes=[pltpu.VMEM((seq.par, GLA_DV, GLA_HEADS * LANES), f32),
                        pltpu.VMEM((seq.par, r, 512), f32)],
        compiler_params=_params(1),
        name="gla",
    )(*args)
    return o.reshape(m, 512), s_new.reshape((seq.batch,) + tail)


def _mlstm_kernel(*refs, seq):
    n_in = 9
    (mu_ref, mo_ref, mif_ref, cw_ref, cb_ref, wq_ref, wk_ref, wv_ref, gn_ref) = refs[:n_in]
    if seq.has_state:
        c0_ref, n0_ref, m0_ref, conv0_ref = refs[n_in:n_in + 4]
        rest = refs[n_in + 4:]
    else:
        c0_ref = n0_ref = m0_ref = conv0_ref = None
        rest = refs[n_in:]
    (o_ref, cout_ref, nout_ref, mout_ref, convout_ref,
     ucat_ref, mcv_sc, q_sc, k_sc, v_sc, c_st, n_st, m_st) = rest
    c = seq.mlstm_chunk
    i = pl.program_id(0)
    heads = range(M_HEADS)
    groups = range(seq.par)
    mask = _tri(c)
    lane = lax.broadcasted_iota(jnp.int32, (c, LANES), 1)
    lane_h = lax.broadcasted_iota(jnp.int32, (1, M_HEADS), 1)

    for p in groups:
        ucat = ucat_ref.at[p]
        mu = mu_ref[p]
        _conv_carry(seq, ucat)
        _conv_store(seq, ucat, mu, conv0_ref, CONV_W, slice(0, M_WIDTH))
        _conv_state_out(seq, ucat, convout_ref.at[p], CONV_W)
        for r0 in range(0, seq.rows, CONV_ROWS):
            for c0 in range(0, M_WIDTH, CONV_COLS):
                cols = slice(c0, c0 + CONV_COLS)
                mcv_sc[p, pl.ds(r0, CONV_ROWS), cols] = _silu(
                    _conv_piece(ucat, cw_ref, cb_ref, r0, CONV_ROWS, cols, CONV_W)).astype(bf16)
        for h in heads:
            sl = slice(h * M_DH, (h + 1) * M_DH)
            q_sc[p, :, sl] = _dot(mcv_sc[p, :, sl], wq_ref[h])
            k_sc[p, :, sl] = _dot(mcv_sc[p, :, sl], wk_ref[h]) * (M_DH ** -0.5)
            v_sc[p, :, sl] = _dot(mu[:, sl], wv_ref[h])

    def chunk(p, rows, c_all, n_all, m_prevs):
        gi = mif_ref[p, rows, :]
        lf = _log_sigmoid(gi)
        if seq.seq_real < c:
            valid_c = lax.broadcasted_iota(jnp.int32, (c, 1), 0) < seq.seq_real
            valid_r = lax.broadcasted_iota(jnp.int32, (1, c), 1) < seq.seq_real
            lf = jnp.where(valid_c, lf, 0.0)
        fcum = _cumsum_rows(lf)
        xt = _first_cols_as_rows(jnp.where(lane < M_HEADS, gi, fcum))
        q = q_sc[p, rows, :]
        k = k_sc[p, rows, :]
        v = v_sc[p, rows, :]
        mo = mo_ref[p, rows, :]
        q_st = _stack_heads(q).astype(bf16)
        qk_st = _dot_nt(q_st, k)
        inter_st = _dot_nt(q_st, jnp.concatenate(
            [c_all, jnp.broadcast_to(n_all, (M_DH, M_WIDTH))], axis=0))
        p_blocks, ws_blocks, per_head = [], [], []
        for h in heads:
            sl = slice(h * M_DH, (h + 1) * M_DH)
            f_col = fcum[:, M_HEADS + h:M_HEADS + h + 1]
            i_col = gi[:, h:h + 1]
            f_row = xt[M_HEADS + h:M_HEADS + h + 1, :]
            i_row = xt[h:h + 1, :]
            if seq.seq_real < c:
                i_col = jnp.where(valid_c, i_col, -jnp.inf)
                i_row = jnp.where(valid_r, i_row, -jnp.inf)
            dm = jnp.where(mask, f_col - f_row + i_row, -jnp.inf)
            b = f_col + m_prevs[h]
            mt = jnp.maximum(b, jnp.max(dm, axis=-1, keepdims=True))
            w_inter = jnp.exp(b - mt)
            pm = jnp.exp(dm - mt) * qk_st[h * c:(h + 1) * c, :]
            m_new = mt[c - 1:c, :]
            dec = jnp.exp(b[c - 1:c, :] - m_new)
            ws = jnp.exp(f_col[c - 1:c, :] - f_col + i_col - m_new)
            p_blocks.append(pm)
            ws_blocks.append(ws)
            per_head.append((w_inter, mt, m_new, dec))
        pv_st = _dot(jnp.concatenate(p_blocks, axis=0),
                     jnp.concatenate([v, jnp.ones((c, M_DH), f32)], axis=1))
        wsv_st = jnp.concatenate([ws_blocks[h] * v[:, h * M_DH:(h + 1) * M_DH] for h in heads],
                                 axis=0)
        dec_all = jnp.concatenate([jnp.broadcast_to(per_head[h][3], (1, M_DH)) for h in heads],
                                  axis=1)
        new_c = dec_all * c_all + _dot_tn(wsv_st, _stack_heads(k))
        new_n = dec_all * n_all + jnp.concatenate(
            [jnp.sum(ws_blocks[h] * k[:, h * M_DH:(h + 1) * M_DH], axis=0, keepdims=True)
             for h in heads], axis=1)
        for h in heads:
            sl = slice(h * M_DH, (h + 1) * M_DH)
            w_inter, mt, _, _ = per_head[h]
            blk = slice(h * c, (h + 1) * c)
            num = w_inter * inter_st[blk, 0:M_DH] + pv_st[blk, sl]
            den = w_inter * inter_st[blk, M_DH:2 * M_DH] + pv_st[blk, M_WIDTH:M_WIDTH + M_DH]
            hh = num / jnp.maximum(jnp.abs(den), jnp.exp(-mt))
            o_ref[p, rows, sl] = _head_rmsnorm(hh, gn_ref[:, sl]) * jax.nn.sigmoid(mo[:, sl])
        return new_c, new_n, [per_head[h][2] for h in heads]

    def store_state(p, idx, c_all, n_all, m_prevs):
        m_row = jnp.zeros((1, M_HEADS), f32)
        for h in heads:
            sl = slice(h * M_DH, (h + 1) * M_DH)
            cout_ref[p, idx, h] = c_all[:, sl]
            nout_ref[p, idx, pl.ds(h, 1), :] = n_all[:, sl]
            m_row = jnp.where(lane_h == h, m_prevs[h], m_row)
        mout_ref[p, idx] = m_row

    if seq.short:
        def body(ci, carry):
            rows = pl.ds(pl.multiple_of(ci * c, c), c)
            if seq.has_state:
                n0 = n0_ref[ci]
                m0 = m0_ref[ci]
                state = (jnp.concatenate([c0_ref[ci, h] for h in heads], axis=1),
                         jnp.concatenate([n0[h:h + 1, :] for h in heads], axis=1),
                         [m0[:, h:h + 1] for h in heads])
            else:
                state = (jnp.zeros((M_DH, M_WIDTH), f32), jnp.zeros((1, M_WIDTH), f32),
                         [jnp.zeros((1, 1), f32)] * M_HEADS)
            store_state(0, ci, *chunk(0, rows, *state))
            return carry

        lax.fori_loop(0, seq.rows // c, body, 0, unroll=SHORT_UNROLL)
    else:
        @pl.when(i % seq.tiles_per_seq == 0)
        def _():
            c_st[...] = jnp.zeros_like(c_st)
            n_st[...] = jnp.zeros_like(n_st)
            m_st[...] = jnp.zeros_like(m_st)

        state = [(c_st[p], n_st[p, 0:1, :], [m_st[p, h:h + 1, 0:1] for h in heads])
                 for p in groups]
        for ci in range(seq.rows // c):
            for p in groups:
                state[p] = chunk(p, pl.ds(ci * c, c), *state[p])
        for p in groups:
            c_st[p] = state[p][0]
            n_st[p, 0:1, :] = state[p][1]
            for h in heads:
                m_st[p, h:h + 1, :] = jnp.broadcast_to(state[p][2][h], (1, LANES))

        @pl.when(i % seq.tiles_per_seq == seq.tiles_per_seq - 1)
        def _():
            for p in groups:
                store_state(p, 0, *state[p])


def _mlstm(proj, cw, cb, wq, wk, wv, gn, states, layer, seq):
    m, n = proj.shape
    r = seq.rows
    par = seq.par
    proj = proj.reshape(par, m // par, n)

    def lw(shape):
        nd = len(shape)
        return pl.BlockSpec((None,) + shape, lambda i: (layer,) + (0,) * nd)

    in_specs = [seq.par_rows_spec(512, COL_MU // 512), seq.par_rows_spec(512, COL_MO // 512),
                seq.par_rows_spec(LANES, COL_MIF // LANES),
                lw((CONV_W, M_WIDTH)), lw((1, M_WIDTH)),
                lw((M_HEADS, M_DH, M_DH)), lw((M_HEADS, M_DH, M_DH)), lw((M_HEADS, M_DH, M_DH)),
                lw((1, M_WIDTH))]
    args = [proj, proj, proj, cw, cb, wq, wk, wv, gn]
    tails = [(M_HEADS, M_DH, M_DH), (M_HEADS, M_DH), (1, M_HEADS), (CONV_W - 1, M_WIDTH)]
    if seq.has_state:
        in_specs += [seq.in_state_spec(t, layer) for t in tails]
        args += list(states)
    outs = pl.pallas_call(
        functools.partial(_mlstm_kernel, seq=seq),
        grid=(seq.n_tiles,),
        in_specs=in_specs,
        out_specs=[seq.par_rows_spec(M_WIDTH, 0)] + [seq.par_state_spec(t) for t in tails],
        out_shape=[jax.ShapeDtypeStruct((par, m // par, M_WIDTH), f32)]
        + [jax.ShapeDtypeStruct((par, seq.batch // par) + t, f32) for t in tails],
        scratch_shapes=[pltpu.VMEM((par, r + SUBLANES, M_WIDTH), f32),
                        pltpu.VMEM((par, r, M_WIDTH), bf16),
                        pltpu.VMEM((par, r, M_WIDTH), f32), pltpu.VMEM((par, r, M_WIDTH), f32),
                        pltpu.VMEM((par, r, M_WIDTH), f32),
                        pltpu.VMEM((par, M_DH, M_WIDTH), f32),
                        pltpu.VMEM((par, SUBLANES, M_WIDTH), f32),
                        pltpu.VMEM((par, SUBLANES, LANES), f32)],
        compiler_params=_params(1),
        name="mlstm",
    )(*args)
    return [outs[0].reshape(m, M_WIDTH)] + [
        o.reshape((seq.batch,) + t) for o, t in zip(outs[1:], tails)]


def _rglru_kernel(*refs, seq):
    n_in = 9
    (rx_ref, ry_ref, cw_ref, cb_ref, wr_ref, br_ref, wi_ref, bi_ref, lam_ref) = refs[:n_in]
    if seq.has_state:
        h0_ref, conv0_ref = refs[n_in:n_in + 2]
        rest = refs[n_in + 2:]
    else:
        h0_ref = conv0_ref = None
        rest = refs[n_in:]
    o_ref, hout_ref, convout_ref, ucat_ref, xc_sc, a_sc, b_sc, h_st = rest
    r = seq.rows
    i = pl.program_id(0)

    _conv_carry(seq, ucat_ref)
    _conv_store(seq, ucat_ref, rx_ref[...], conv0_ref, CONV_W, slice(0, R_WIDTH))
    _conv_state_out(seq, ucat_ref, convout_ref, CONV_W)
    pieces = [(r0, slice(c0, c0 + CONV_COLS))
              for r0 in range(0, r, CONV_ROWS) for c0 in range(0, R_WIDTH, CONV_COLS)]
    for r0, cols in pieces:
        xc_sc[pl.ds(r0, CONV_ROWS), cols] = _conv_piece(ucat_ref, cw_ref, cb_ref, r0, CONV_ROWS,
                                                        cols, CONV_W)
    xc_bf = xc_sc[...].astype(bf16)
    a_sc[...] = jnp.dot(xc_bf, wr_ref[...], preferred_element_type=f32) + br_ref[...]
    b_sc[...] = jnp.dot(xc_bf, wi_ref[...], preferred_element_type=f32) + bi_ref[...]
    log_lam = _log_sigmoid(lam_ref[...])
    for r0, cols in pieces:
        rows = pl.ds(r0, CONV_ROWS)
        log_a = R_C * jax.nn.sigmoid(a_sc[rows, cols]) * log_lam[:, cols]
        a = jnp.exp(log_a)
        gated_x = jax.nn.sigmoid(b_sc[rows, cols]) * xc_sc[rows, cols]
        a_sc[rows, cols] = a
        b_sc[rows, cols] = jnp.sqrt(-jnp.tanh(log_a) * (a * a + 1.0)) * gated_x

    if not seq.short:
        @pl.when(i % seq.tiles_per_seq == 0)
        def _():
            h_st[...] = jnp.zeros_like(h_st)

    sub = lax.broadcasted_iota(jnp.int32, (SUBLANES, R_WIDTH), 0)

    def body(gi, h_prev):
        rows = pl.ds(pl.multiple_of(gi * SUBLANES, SUBLANES), SUBLANES)
        ag = a_sc[rows, :]
        bg = b_sc[rows, :]
        for d in (1, 2, 4):
            keep = sub >= d
            bg = jnp.where(keep, ag * pltpu.roll(bg, d, 0) + bg, bg)
            ag = jnp.where(keep, ag * pltpu.roll(ag, d, 0), ag)
        if seq.short:
            if seq.has_state:
                h_in = h0_ref[gi]
            else:
                h_in = jnp.zeros((1, R_WIDTH), f32)
        else:
            h_in = h_prev
        hg = bg + ag * h_in
        o_ref[rows, :] = hg * _gelu_tanh(ry_ref[rows, :])
        if seq.short:
            hout_ref[gi] = hg[seq.seq_real - 1:seq.seq_real, :]
            return h_prev
        return hg[SUBLANES - 1:SUBLANES, :]

    h_last = lax.fori_loop(0, r // SUBLANES, body, h_st[0:1, :])

    if not seq.short:
        h_st[0:1, :] = h_last

        @pl.when(i % seq.tiles_per_seq == seq.tiles_per_seq - 1)
        def _():
            hout_ref[0] = h_last


def _rglru(proj, cw, cb, wr, br, wi, bi, lam, states, layer, seq):
    m = proj.shape[0]
    r = seq.rows

    def col(cb_, w):
        return pl.BlockSpec((r, w), lambda i: (i, cb_))

    def lw(shape):
        n = len(shape)
        return pl.BlockSpec((None,) + shape, lambda i: (layer,) + (0,) * n)

    in_specs = [col(COL_RX // 512, 512), col(COL_RY // 512, 512),
                lw((CONV_W, R_WIDTH)), lw((1, R_WIDTH)),
                lw((R_WIDTH, R_WIDTH)), lw((1, R_WIDTH)),
                lw((R_WIDTH, R_WIDTH)), lw((1, R_WIDTH)), lw((1, R_WIDTH))]
    args = [proj, proj, cw, cb, wr, br, wi, bi, lam]
    tails = [(1, R_WIDTH), (CONV_W - 1, R_WIDTH)]
    if seq.has_state:
        in_specs += [seq.in_state_spec(t, layer) for t in tails]
        args += list(states)
    return pl.pallas_call(
        functools.partial(_rglru_kernel, seq=seq),
        grid=(seq.n_tiles,),
        in_specs=in_specs,
        out_specs=[pl.BlockSpec((r, R_WIDTH), lambda i: (i, 0))]
        + [seq.state_spec(t) for t in tails],
        out_shape=[jax.ShapeDtypeStruct((m, R_WIDTH), f32)]
        + [jax.ShapeDtypeStruct((seq.batch,) + t, f32) for t in tails],
        scratch_shapes=[pltpu.VMEM((r + SUBLANES, R_WIDTH), f32), pltpu.VMEM((r, R_WIDTH), f32),
                        pltpu.VMEM((r, R_WIDTH), f32), pltpu.VMEM((r, R_WIDTH), f32),
                        pltpu.VMEM((SUBLANES, R_WIDTH), f32)],
        compiler_params=_params(1),
        name="rglru",
    )(*args)


def _merge_ffn_kernel(*refs, seq, final_norm):
    n_in = 13
    (gates_ref, og_ref, om_ref, or_ref, x_ref, wb_ref, wo_ref,
     g_ref, wu_ref, cw_ref, cb_ref, wd_ref, gf_ref) = refs[:n_in]
    if seq.has_state:
        conv0_ref = refs[n_in]
        rest = refs[n_in + 1:]
    else:
        conv0_ref = None
        rest = refs[n_in:]
    o_ref, convout_ref, ucat_ref, act_sc, mix_sc = rest

    branches = (og_ref, om_ref, or_ref)
    o_bf = [b_ref[...].astype(bf16) for b_ref in branches]
    for c0 in range(0, D_MODEL, CONV_COLS):
        mix = None
        for j in range(len(branches)):
            gate = jax.nn.sigmoid(gates_ref[:, j * D_MODEL + c0:j * D_MODEL + c0 + CONV_COLS])
            term = gate * jnp.dot(o_bf[j], wb_ref[j * 512:(j + 1) * 512, c0:c0 + CONV_COLS],
                                  preferred_element_type=f32)
            mix = term if mix is None else mix + term
        mix_sc[:, c0:c0 + CONV_COLS] = mix.astype(bf16)
    x = x_ref[...] + jnp.dot(mix_sc[...], wo_ref[...], preferred_element_type=f32)

    xn = (x * lax.rsqrt(jnp.mean(x * x, axis=-1, keepdims=True) + EPS) * g_ref[...]).astype(bf16)
    _conv_carry(seq, ucat_ref)
    y = x
    for g0 in range(0, D_FF, FFN_GROUP):
        g1 = min(g0 + FFN_GROUP, D_FF)
        for c0 in range(g0, g1, CONV_COLS):
            halves = [slice(base + c0, base + c0 + CONV_COLS) for base in (0, D_FF)]
            for cols in halves:
                u = jnp.dot(xn, wu_ref[:, cols], preferred_element_type=f32)
                _conv_store(seq, ucat_ref, u, conv0_ref, FFN_CONV_W, cols)
            for r0 in range(0, seq.rows, CONV_ROWS):
                for p0 in range(c0, c0 + CONV_COLS, LANES):
                    u_a, u_b = (_conv_piece(ucat_ref, cw_ref, cb_ref, r0, CONV_ROWS,
                                            slice(base + p0, base + p0 + LANES), FFN_CONV_W)
                                for base in (0, D_FF))
                    act_sc[pl.ds(r0, CONV_ROWS), p0:p0 + LANES] = (
                        _gelu_tanh(u_a) * u_b).astype(bf16)
        y = y + jnp.dot(act_sc[:, g0:g1], wd_ref[g0:g1, :], preferred_element_type=f32)
    _conv_state_out(seq, ucat_ref, convout_ref, FFN_CONV_W)
    if final_norm:
        y = y * lax.rsqrt(jnp.mean(y * y, axis=-1, keepdims=True) + EPS) * gf_ref[...]
    o_ref[...] = y


def _merge_ffn(proj, o_g, o_m, o_r, x, wts, conv0, layer, seq, final_norm):
    m = x.shape[0]
    r = seq.rows
    row = lambda w: pl.BlockSpec((r, w), lambda i: (i, 0))

    def lw(shape, **kw):
        n = len(shape)
        return pl.BlockSpec((None,) + shape, lambda i: (layer,) + (0,) * n, **kw)

    once = dict(pipeline_mode=pl.Buffered(1))
    in_specs = [row(3 * D_MODEL), row(512), row(512), row(512), row(D_MODEL),
                lw((1536, D_MODEL), **once), lw((D_MODEL, D_MODEL), **once),
                lw((1, D_MODEL)), lw((D_MODEL, 2 * D_FF), **once),
                lw((FFN_CONV_W, 2 * D_FF)), lw((1, 2 * D_FF)), lw((D_FF, D_MODEL), **once),
                pl.BlockSpec((1, D_MODEL), lambda i: (0, 0))]
    args = [proj, o_g, o_m, o_r, x, wts['w_branch'], wts['w_out'], wts['norm_ffn_g'], wts['w_up'],
            wts['ffn_conv_w'], wts['ffn_conv_b'], wts['w_down'], wts['norm_f_g']]
    tail = (FFN_CONV_W - 1, 2 * D_FF)
    if seq.has_state:
        in_specs.append(seq.in_state_spec(tail, layer))
        args.append(conv0)
    return pl.pallas_call(
        functools.partial(_merge_ffn_kernel, seq=seq, final_norm=final_norm),
        grid=(seq.n_tiles,),
        in_specs=in_specs,
        out_specs=[row(D_MODEL), seq.state_spec(tail)],
        out_shape=[jax.ShapeDtypeStruct((m, D_MODEL), f32),
                   jax.ShapeDtypeStruct((seq.batch,) + tail, f32)],
        scratch_shapes=[pltpu.VMEM((r + SUBLANES, 2 * D_FF), f32), pltpu.VMEM((r, D_FF), bf16),
                        pltpu.VMEM((r, D_MODEL), bf16)],
        compiler_params=_params(1),
        name="merge_ffn",
    )(*args)


def _head_pad(a):
    lead = a.shape[:-1]
    a = a.reshape(lead + (GLA_HEADS, GLA_DK))
    a = jnp.pad(a, [(0, 0)] * len(lead) + [(0, 0), (0, LANES - GLA_DK)])
    return a.reshape(lead + (GLA_HEADS * LANES,))


def _lane_pad(a, width):
    return jnp.pad(a, [(0, 0)] * (a.ndim - 1) + [(0, width - a.shape[-1])])


def _permute_in_cols(a):
    sizes = (256, 256, 512, 512, GLA_RANK, 512, M_HEADS, M_HEADS, 512, 512, 512, 3 * D_MODEL)
    parts = []
    off = 0
    for s in sizes:
        parts.append(a[..., off:off + s])
        off += s
    gq, gk, gv, gg, ga, mu, mi, mf, mo, rx, ry, gates = parts
    out = jnp.concatenate(
        [gates, _head_pad(gq), _head_pad(gk), gv, gg, mu, mo, rx, ry,
         _lane_pad(ga, LANES), _lane_pad(jnp.concatenate([mi, mf], axis=-1), LANES)], axis=-1)
    return _lane_pad(out, D_IN_PAD)


def _block_diag(w):
    d, nb, bs, _ = w.shape
    eye = jnp.eye(nb, dtype=w.dtype)
    return jnp.einsum('lnde,nm->lndme', w, eye).reshape(d, nb * bs, nb * bs)


def _run_group(x, seq_pair, seq_mix, seq_ffn, states, wts, tm):
    per_layer = []
    for l in range(DEPTH):
        proj = _in_proj(x, wts['norm_mix_g'], wts['w_in'], wts['b_in'], l, tm=tm, tn=2560)
        st = states
        o_g, s_new = _gla(proj, wts['w_gla_a2'], wts['b_gla_a2'], wts['gla_norm_g'],
                          st[0] if st else None, l, seq_pair)
        o_m, c_new, n_new, m_new, mconv_new = _mlstm(
            proj, wts['mlstm_conv_w'], wts['mlstm_conv_b'], wts['w_mlstm_q'], wts['w_mlstm_k'],
            wts['w_mlstm_v'], wts['mlstm_norm_g'], st[1:5] if st else None, l, seq_pair)
        o_r, h_new, rconv_new = _rglru(
            proj, wts['rglru_conv_w'], wts['rglru_conv_b'], wts['w_rglru_r'], wts['b_rglru_r'],
            wts['w_rglru_i'], wts['b_rglru_i'], wts['rglru_lambda'], st[5:7] if st else None,
            l, seq_mix)
        x, fconv_new = _merge_ffn(proj, o_g, o_m, o_r, x, wts, st[7] if st else None, l, seq_ffn,
                                  final_norm=(l == DEPTH - 1))
        per_layer.append([s_new, c_new, n_new, m_new, mconv_new, h_new, rconv_new, fconv_new])
    new_states = [jnp.stack([st[j] for st in per_layer]) for j in range(8)]
    return x, new_states


def kernel(x_prompt, x_sample, state_gla_S, state_mlstm_C, state_mlstm_n, state_mlstm_m,
           state_mlstm_conv, state_rglru_h, state_rglru_conv, state_ffn_conv,
           norm_mix_g, w_in, b_in, w_gla_a2, b_gla_a2, gla_norm_g,
           mlstm_conv_w, mlstm_conv_b, w_mlstm_q, w_mlstm_k, w_mlstm_v, mlstm_norm_g,
           rglru_conv_w, rglru_conv_b, w_rglru_r, b_rglru_r, w_rglru_i, b_rglru_i, rglru_lambda,
           w_branch, w_out, norm_ffn_g, w_up, ffn_conv_w, ffn_conv_b, w_down, norm_f_g):
    row = lambda a: a[:, None, :]
    wts = {
        'norm_mix_g': row(norm_mix_g),
        'w_in': _permute_in_cols(w_in).astype(bf16),
        'b_in': row(_permute_in_cols(b_in)),
        'w_gla_a2': _head_pad(jnp.pad(w_gla_a2, ((0, 0), (0, LANES - GLA_RANK), (0, 0)))).astype(bf16),
        'b_gla_a2': row(_head_pad(b_gla_a2)),
        'gla_norm_g': row(gla_norm_g),
        'mlstm_conv_w': mlstm_conv_w, 'mlstm_conv_b': row(mlstm_conv_b),
        'w_mlstm_q': w_mlstm_q.astype(bf16), 'w_mlstm_k': w_mlstm_k.astype(bf16),
        'w_mlstm_v': w_mlstm_v.astype(bf16), 'mlstm_norm_g': row(mlstm_norm_g),
        'rglru_conv_w': rglru_conv_w, 'rglru_conv_b': row(rglru_conv_b),
        'w_rglru_r': _block_diag(w_rglru_r).astype(bf16), 'b_rglru_r': row(b_rglru_r),
        'w_rglru_i': _block_diag(w_rglru_i).astype(bf16), 'b_rglru_i': row(b_rglru_i),
        'rglru_lambda': row(rglru_lambda),
        'w_branch': w_branch.astype(bf16), 'w_out': w_out.astype(bf16),
        'norm_ffn_g': row(norm_ffn_g), 'w_up': w_up.astype(bf16),
        'ffn_conv_w': ffn_conv_w, 'ffn_conv_b': row(ffn_conv_b),
        'w_down': w_down.astype(bf16), 'norm_f_g': norm_f_g[None, :],
    }

    bp, lp, _ = x_prompt.shape
    seq_p_pair = _Seq(bp, lp, lp, 512, has_state=False, par=2)
    seq_p_mix = _Seq(bp, lp, lp, 512, has_state=False)
    seq_p_ffn = _Seq(bp, lp, lp, 256, has_state=False)
    y_p, p_states = _run_group(x_prompt.reshape(bp * lp, D_MODEL), seq_p_pair, seq_p_mix,
                               seq_p_ffn, None, wts, tm=1024)
    y_prompt = y_p.reshape(bp, lp, D_MODEL)

    bs, ls, _ = x_sample.shape
    xs = jnp.pad(x_sample, ((0, 0), (0, SUBLANES - ls), (0, 0))).reshape(bs * SUBLANES, D_MODEL)
    seq_s = _Seq(bs, SUBLANES, ls, 128, has_state=True)
    s_in = [state_gla_S, state_mlstm_C, state_mlstm_n, state_mlstm_m[:, :, None, :],
            state_mlstm_conv, state_rglru_h[:, :, None, :], state_rglru_conv, state_ffn_conv]
    y_s, s_states = _run_group(xs, seq_s, seq_s, seq_s, s_in, wts, tm=1024)
    y_sample = y_s.reshape(bs, SUBLANES, D_MODEL)[:, :ls]

    def unpack(st):
        s, c, n, m, mconv, h, rconv, fconv = st
        return (s, c, n, m[:, :, 0, :], mconv, h[:, :, 0, :], rconv, fconv)

    return (y_prompt, y_sample) + unpack(p_states) + unpack(s_states)
```

```python
import functools

import jax
import jax.numpy as jnp
from jax import lax
from jax.experimental import pallas as pl
from jax.experimental.pallas import tpu as pltpu

f32 = jnp.float32
bf16 = jnp.bfloat16

D_MODEL = 1024
DEPTH = 4
GLA_HEADS = 4
GLA_DK = 64
GLA_DV = 128
GLA_RANK = 16
GLA_TAU = 16.0
M_HEADS = 4
M_DH = 128
M_WIDTH = 512
CONV_W = 4
R_WIDTH = 512
R_BLOCKS = 8
R_BS = 64
R_C = 8.0
D_FF = 2816
FFN_CONV_W = 3
EPS = 1e-6

LANES = 128
SUBLANES = 8
GLA_CHUNK = 64
MLSTM_CHUNK = 256
SHORT_UNROLL = 4
GROUP_UNROLL = 8
CONV_ROWS = 64
CONV_COLS = 256
FFN_GROUP = 1024
VMEM_LIMIT = 52 * 1024 * 1024

COL_GATES = 0
COL_GQ = 3072
COL_GK = 3584
COL_GV = 4096
COL_GG = 4608
COL_MU = 5120
COL_MO = 5632
COL_RX = 6144
COL_RY = 6656
COL_GA = 7168
COL_MIF = 7296
D_IN_PAD = 7680


def _log_sigmoid(x):
    return jnp.minimum(x, 0.0) - jnp.log(1.0 + jnp.exp(-jnp.abs(x)))


def _silu(x):
    return x * jax.nn.sigmoid(x)


def _gelu_tanh(x):
    hx = 0.5 * x
    return hx + hx * jnp.tanh(x * (0.7978845608028654 + (0.7978845608028654 * 0.044715) * (x * x)))


def _head_rmsnorm(o, g):
    return o * lax.rsqrt(jnp.mean(o * o, axis=-1, keepdims=True) + EPS) * g


def _dot(a, b):
    return jnp.dot(a.astype(bf16), b.astype(bf16), preferred_element_type=f32)


def _dot_nt(a, b):
    return lax.dot_general(a.astype(bf16), b.astype(bf16), (((1,), (1,)), ((), ())),
                           preferred_element_type=f32)


def _dot_tn(a, b):
    return lax.dot_general(a.astype(bf16), b.astype(bf16), (((0,), (0,)), ((), ())),
                           preferred_element_type=f32)


def _bf16_parts(x):
    p0 = x.astype(bf16).astype(f32)
    r0 = x - p0
    p1 = r0.astype(bf16).astype(f32)
    p2 = (r0 - p1).astype(bf16).astype(f32)
    return p0, p1, p2


def _cumsum_rows(x):
    c, n = x.shape
    if c == SUBLANES:
        sub = lax.broadcasted_iota(jnp.int32, x.shape, 0)
        for d in (1, 2, 4):
            x = x + jnp.where(sub >= d, pltpu.roll(x, d, 0), 0.0)
        return x
    y = _dot(_tri(c).astype(f32), jnp.concatenate(_bf16_parts(x), axis=1))
    return (y[:, :n] + y[:, n:2 * n]) + y[:, 2 * n:]


def _first_cols_as_rows(x):
    c = x.shape[0]
    cp = max(c, LANES)
    sel = (lax.broadcasted_iota(jnp.int32, (SUBLANES, LANES), 0)
           == lax.broadcasted_iota(jnp.int32, (SUBLANES, LANES), 1)).astype(f32)
    pad = [jnp.zeros((cp - c, LANES), f32)] if cp > c else []
    stacked = jnp.concatenate([piece for p in _bf16_parts(x) for piece in [p] + pad], axis=0)
    y = _dot_nt(sel, stacked)
    return (y[:, 0:c] + y[:, cp:cp + c]) + y[:, 2 * cp:2 * cp + c]


def _tri(c):
    r = lax.broadcasted_iota(jnp.int32, (c, c), 0)
    s = lax.broadcasted_iota(jnp.int32, (c, c), 1)
    return r >= s


def _stack_heads(x):
    lane = lax.broadcasted_iota(jnp.int32, (1, x.shape[1]), 1)
    blocks = [jnp.where((lane >= h * LANES) & (lane < (h + 1) * LANES), x, 0.0)
              for h in range(x.shape[1] // LANES)]
    return jnp.concatenate(blocks, axis=0)


def _split_heads(x):
    return jnp.concatenate([x[:, h * LANES:(h + 1) * LANES] for h in range(x.shape[1] // LANES)],
                           axis=0)


def _params(n_axes):
    return pltpu.CompilerParams(dimension_semantics=("arbitrary",) * n_axes,
                                vmem_limit_bytes=VMEM_LIMIT)


def _in_proj_kernel(x_ref, g_ref, w_ref, b_ref, o_ref, xn_ref):
    @pl.when(pl.program_id(1) == 0)
    def _():
        x = x_ref[...]
        ms = jnp.mean(x * x, axis=-1, keepdims=True)
        xn_ref[...] = (x * lax.rsqrt(ms + EPS) * g_ref[...]).astype(bf16)

    o_ref[...] = jnp.dot(xn_ref[...], w_ref[...], preferred_element_type=f32) + b_ref[...]


def _in_proj(x, g, w, b, layer, *, tm, tn):
    m, d = x.shape
    n = w.shape[-1]
    return pl.pallas_call(
        _in_proj_kernel,
        grid=(m // tm, n // tn),
        in_specs=[pl.BlockSpec((tm, d), lambda i, j: (i, 0)),
                  pl.BlockSpec((None, 1, d), lambda i, j: (layer, 0, 0)),
                  pl.BlockSpec((None, d, tn), lambda i, j: (layer, 0, j)),
                  pl.BlockSpec((None, 1, tn), lambda i, j: (layer, 0, j))],
        out_specs=pl.BlockSpec((tm, tn), lambda i, j: (i, j)),
        out_shape=jax.ShapeDtypeStruct((m, n), f32),
        scratch_shapes=[pltpu.VMEM((tm, d), bf16)],
        compiler_params=_params(2),
        name="in_proj",
    )(x, g, w, b)


class _Seq:
    def __init__(self, batch, seq_pad, seq_real, rows, has_state, par=1):
        assert batch % par == 0
        self.par = par
        self.batch = batch
        self.seq_pad = seq_pad
        self.seq_real = seq_real
        self.rows = rows
        self.has_state = has_state
        self.short = seq_pad <= rows
        if self.short:
            assert rows % seq_pad == 0 and seq_pad == SUBLANES
            self.seqs_per_tile = rows // seq_pad
            self.tiles_per_seq = 1
            self.gla_chunk = self.mlstm_chunk = seq_pad
        else:
            assert seq_pad % rows == 0 and seq_pad == seq_real
            self.seqs_per_tile = 1
            self.tiles_per_seq = seq_pad // rows
            self.gla_chunk = GLA_CHUNK
            self.mlstm_chunk = min(MLSTM_CHUNK, rows)
        self.n_tiles = batch * seq_pad // rows // par

    def state_spec(self, tail):
        nt = len(tail)
        tps = self.tiles_per_seq
        return pl.BlockSpec((self.seqs_per_tile,) + tuple(tail),
                            lambda i: (i // tps,) + (0,) * nt)

    def par_state_spec(self, tail):
        nt = len(tail)
        tps = self.tiles_per_seq
        return pl.BlockSpec((self.par, self.seqs_per_tile) + tuple(tail),
                            lambda i: (0, i // tps) + (0,) * nt)

    def par_rows_spec(self, width, col_block):
        return pl.BlockSpec((self.par, self.rows, width), lambda i: (0, i, col_block))

    def in_state_spec(self, tail, layer):
        nt = len(tail)
        return pl.BlockSpec((None, self.seqs_per_tile) + tuple(tail),
                            lambda i: (layer, i) + (0,) * nt)


def _conv_carry(seq, ucat_ref):
    if seq.short:
        return
    r = seq.rows
    i = pl.program_id(0)
    width = ucat_ref.shape[-1]

    @pl.when(i % seq.tiles_per_seq == 0)
    def _():
        ucat_ref[pl.ds(0, SUBLANES), :] = jnp.zeros((SUBLANES, width), f32)

    @pl.when(i % seq.tiles_per_seq != 0)
    def _():
        ucat_ref[pl.ds(0, SUBLANES), :] = ucat_ref[pl.ds(r, SUBLANES), :]


def _conv_store(seq, ucat_ref, u, state_ref, width, cols):
    ucat_ref[pl.ds(SUBLANES, seq.rows), cols] = u
    if seq.short:
        for s in range(seq.seqs_per_tile):
            lo = SUBLANES + s * seq.seq_pad - (width - 1)
            if seq.has_state:
                ucat_ref[pl.ds(lo, width - 1), cols] = state_ref[s, :, cols]
            else:
                ucat_ref[pl.ds(lo, width - 1), cols] = jnp.zeros((width - 1, u.shape[-1]), f32)


def _conv_piece(ucat_ref, w_ref, b_ref, r0, rows, cols, width):
    u = ucat_ref[pl.ds(r0, rows + SUBLANES), cols]
    s = u * w_ref[0:1, cols]
    for j in range(1, width - 1):
        s = u * w_ref[j:j + 1, cols] + _shift_rows(s)
    y = (b_ref[:, cols] + u * w_ref[width - 1:width, cols]) + _shift_rows(s)
    return y[SUBLANES:, :]


def _shift_rows(x):
    n_tiles = x.shape[0] // SUBLANES
    first = lax.broadcasted_iota(jnp.int32, (SUBLANES, x.shape[1]), 0) == 0
    rot = [pltpu.roll(x[t * SUBLANES:(t + 1) * SUBLANES, :], 1, 0) for t in range(n_tiles)]
    return jnp.concatenate([jnp.where(first, rot[t - 1], rot[t]) for t in range(n_tiles)], axis=0)


def _conv_state_out(seq, ucat_ref, out_ref, width):
    r = seq.rows
    i = pl.program_id(0)
    if seq.short:
        for s in range(seq.seqs_per_tile):
            lo = SUBLANES + s * seq.seq_pad + seq.seq_real - (width - 1)
            out_ref[s] = ucat_ref[pl.ds(lo, width - 1), :]
    else:
        @pl.when(i % seq.tiles_per_seq == seq.tiles_per_seq - 1)
        def _():
            out_ref[0] = ucat_ref[pl.ds(SUBLANES + r - (width - 1), width - 1), :]


def _gla_kernel(*refs, seq):
    if seq.has_state:
        (q_ref, k_ref, v_ref, g_ref, a_ref, wa_ref, ba_ref, gn_ref, s0_ref,
         o_ref, sout_ref, st_ref, lg_sc) = refs
    else:
        (q_ref, k_ref, v_ref, g_ref, a_ref, wa_ref, ba_ref, gn_ref,
         o_ref, sout_ref, st_ref, lg_sc) = refs
        s0_ref = None
    c = seq.gla_chunk
    i = pl.program_id(0)
    heads = range(GLA_HEADS)
    groups = range(seq.par)
    mask4 = ((lax.broadcasted_iota(jnp.int32, (GLA_HEADS * c, c), 0) & (c - 1))
             >= lax.broadcasted_iota(jnp.int32, (GLA_HEADS * c, c), 1))
    zpad = jnp.zeros((LANES - GLA_DK, GLA_DV), f32)

    for p in groups:
        for r0 in range(0, seq.rows, LANES):
            rows = pl.ds(r0, LANES)
            lg_sc[p, rows, :] = _log_sigmoid(
                _dot(a_ref[p, rows, :], wa_ref[...]) + ba_ref[...]) / GLA_TAU

    def chunk(p, rows, st):
        q = q_ref[p, rows, :] * (GLA_DK ** -0.5)
        k = k_ref[p, rows, :]
        v = v_ref[p, rows, :]
        g = g_ref[p, rows, :]
        lg = lg_sc[p, rows, :]
        if seq.seq_real < c:
            valid = lax.broadcasted_iota(jnp.int32, (c, 1), 0) < seq.seq_real
            lg = jnp.where(valid, lg, 0.0)
            k = jnp.where(valid, k, 0.0)
        bcum = _cumsum_rows(lg)
        blast = bcum[c - 1:c, :]
        q_d = q * jnp.exp(bcum)
        k_d = k * jnp.exp(-bcum)
        dec = jnp.exp(blast)
        k_e = k_d * dec
        qd_st = _stack_heads(q_d).astype(bf16)
        a_st = jnp.where(mask4, _dot_nt(qd_st, k_d), 0.0)
        o_intra = _dot(a_st, v)
        o_inter = _dot_nt(qd_st, st)
        new_st = dec * st + _dot_tn(_split_heads(v), _stack_heads(k_e))
        for h in heads:
            sl = slice(h * LANES, (h + 1) * LANES)
            o_h = o_intra[h * c:(h + 1) * c, sl] + o_inter[h * c:(h + 1) * c, :]
            o_ref[p, rows, sl] = _head_rmsnorm(o_h, gn_ref[:, sl]) * _silu(g[:, sl])
        return new_st

    def state_out(p, idx, st):
        for h in heads:
            sout_ref[p, idx, h] = st[:, h * LANES:(h + 1) * LANES].T[0:GLA_DK, :]

    if seq.short:
        def body(ci, carry):
            rows = pl.ds(pl.multiple_of(ci * c, c), c)
            if seq.has_state:
                st = jnp.concatenate(
                    [jnp.concatenate([s0_ref[ci, h], zpad], axis=0).T for h in heads], axis=1)
            else:
                st = jnp.zeros((GLA_DV, GLA_HEADS * LANES), f32)
            state_out(0, ci, chunk(0, rows, st))
            return carry

        lax.fori_loop(0, seq.rows // c, body, 0, unroll=SHORT_UNROLL)
    else:
        @pl.when(i % seq.tiles_per_seq == 0)
        def _():
            st_ref[...] = jnp.zeros_like(st_ref)

        st = [st_ref[p] for p in groups]
        for ci in range(seq.rows // c):
            for p in groups:
                st[p] = chunk(p, pl.ds(ci * c, c), st[p])
        for p in groups:
            st_ref[p] = st[p]

        @pl.when(i % seq.tiles_per_seq == seq.tiles_per_seq - 1)
        def _():
            for p in groups:
                state_out(p, 0, st[p])


def _gla(proj, wa, ba, gn, s0, layer, seq):
    m, n = proj.shape
    r = seq.rows
    proj = proj.reshape(seq.par, m // seq.par, n)
    in_specs = [seq.par_rows_spec(512, COL_GQ // 512), seq.par_rows_spec(512, COL_GK // 512),
                seq.par_rows_spec(512, COL_GV // 512), seq.par_rows_spec(512, COL_GG // 512),
                seq.par_rows_spec(LANES, COL_GA // LANES),
                pl.BlockSpec((None, LANES, 512), lambda i: (layer, 0, 0)),
                pl.BlockSpec((None, 1, 512), lambda i: (layer, 0, 0)),
                pl.BlockSpec((None, 1, 512), lambda i: (layer, 0, 0))]
    args = [proj, proj, proj, proj, proj, wa, ba, gn]
    tail = (GLA_HEADS, GLA_DK, GLA_DV)
    if seq.has_state:
        in_specs.append(seq.in_state_spec(tail, layer))
        args.append(s0)
    o, s_new = pl.pallas_call(
        functools.partial(_gla_kernel, seq=seq),
        grid=(seq.n_tiles,),
        in_specs=in_specs,
        out_specs=[seq.par_rows_spec(512, 0), seq.par_state_spec(tail)],
        out_shape=[jax.ShapeDtypeStruct((seq.par, m // seq.par, 512), f32),
                   jax.ShapeDtypeStruct((seq.par, seq.batch // seq.par) + tail, f32)],
        scratch_shapes=[pltpu.VMEM((seq.par, GLA_DV, GLA_HEADS * LANES), f32),
                        pltpu.VMEM((seq.par, r, 512), f32)],
        compiler_params=_params(1),
        name="gla",
    )(*args)
    return o.reshape(m, 512), s_new.reshape((seq.batch,) + tail)


def _mlstm_kernel(*refs, seq):
    n_in = 9
    (mu_ref, mo_ref, mif_ref, cw_ref, cb_ref, wq_ref, wk_ref, wv_ref, gn_ref) = refs[:n_in]
    if seq.has_state:
        c0_ref, n0_ref, m0_ref, conv0_ref = refs[n_in:n_in + 4]
        rest = refs[n_in + 4:]
    else:
        c0_ref = n0_ref = m0_ref = conv0_ref = None
        rest = refs[n_in:]
    (o_ref, cout_ref, nout_ref, mout_ref, convout_ref,
     ucat_ref, mcv_sc, q_sc, k_sc, v_sc, c_st, n_st, m_st) = rest
    c = seq.mlstm_chunk
    i = pl.program_id(0)
    heads = range(M_HEADS)
    groups = range(seq.par)
    mask = _tri(c)
    lane = lax.broadcasted_iota(jnp.int32, (c, LANES), 1)
    lane_h = lax.broadcasted_iota(jnp.int32, (1, M_HEADS), 1)

    for p in groups:
        ucat = ucat_ref.at[p]
        mu = mu_ref[p]
        _conv_carry(seq, ucat)
        _conv_store(seq, ucat, mu, conv0_ref, CONV_W, slice(0, M_WIDTH))
        _conv_state_out(seq, ucat, convout_ref.at[p], CONV_W)
        for r0 in range(0, seq.rows, CONV_ROWS):
            for c0 in range(0, M_WIDTH, CONV_COLS):
                cols = slice(c0, c0 + CONV_COLS)
                mcv_sc[p, pl.ds(r0, CONV_ROWS), cols] = _silu(
                    _conv_piece(ucat, cw_ref, cb_ref, r0, CONV_ROWS, cols, CONV_W)).astype(bf16)
        for h in heads:
            sl = slice(h * M_DH, (h + 1) * M_DH)
            q_sc[p, :, sl] = _dot(mcv_sc[p, :, sl], wq_ref[h])
            k_sc[p, :, sl] = _dot(mcv_sc[p, :, sl], wk_ref[h]) * (M_DH ** -0.5)
            v_sc[p, :, sl] = _dot(mu[:, sl], wv_ref[h])

    def chunk(p, rows, c_all, n_all, m_prevs):
        gi = mif_ref[p, rows, :]
        lf = _log_sigmoid(gi)
        if seq.seq_real < c:
            valid_c = lax.broadcasted_iota(jnp.int32, (c, 1), 0) < seq.seq_real
            valid_r = lax.broadcasted_iota(jnp.int32, (1, c), 1) < seq.seq_real
            lf = jnp.where(valid_c, lf, 0.0)
        fcum = _cumsum_rows(lf)
        xt = _first_cols_as_rows(jnp.where(lane < M_HEADS, gi, fcum))
        q = q_sc[p, rows, :]
        k = k_sc[p, rows, :]
        v = v_sc[p, rows, :]
        mo = mo_ref[p, rows, :]
        q_st = _stack_heads(q).astype(bf16)
        qk_st = _dot_nt(q_st, k)
        inter_st = _dot_nt(q_st, jnp.concatenate(
            [c_all, jnp.broadcast_to(n_all, (M_DH, M_WIDTH))], axis=0))
        p_blocks, ws_blocks, per_head = [], [], []
        for h in heads:
            sl = slice(h * M_DH, (h + 1) * M_DH)
            f_col = fcum[:, M_HEADS + h:M_HEADS + h + 1]
            i_col = gi[:, h:h + 1]
            f_row = xt[M_HEADS + h:M_HEADS + h + 1, :]
            i_row = xt[h:h + 1, :]
            if seq.seq_real < c:
                i_col = jnp.where(valid_c, i_col, -jnp.inf)
                i_row = jnp.where(valid_r, i_row, -jnp.inf)
            dm = jnp.where(mask, f_col - f_row + i_row, -jnp.inf)
            b = f_col + m_prevs[h]
            mt = jnp.maximum(b, jnp.max(dm, axis=-1, keepdims=True))
            w_inter = jnp.exp(b - mt)
            pm = jnp.exp(dm - mt) * qk_st[h * c:(h + 1) * c, :]
            m_new = mt[c - 1:c, :]
            dec = jnp.exp(b[c - 1:c, :] - m_new)
            ws = jnp.exp(f_col[c - 1:c, :] - f_col + i_col - m_new)
            p_blocks.append(pm)
            ws_blocks.append(ws)
            per_head.append((w_inter, mt, m_new, dec))
        pv_st = _dot(jnp.concatenate(p_blocks, axis=0),
                     jnp.concatenate([v, jnp.ones((c, M_DH), f32)], axis=1))
        wsv_st = jnp.concatenate([ws_blocks[h] * v[:, h * M_DH:(h + 1) * M_DH] for h in heads],
                                 axis=0)
        dec_all = jnp.concatenate([jnp.broadcast_to(per_head[h][3], (1, M_DH)) for h in heads],
                                  axis=1)
        new_c = dec_all * c_all + _dot_tn(wsv_st, _stack_heads(k))
        new_n = dec_all * n_all + jnp.concatenate(
            [jnp.sum(ws_blocks[h] * k[:, h * M_DH:(h + 1) * M_DH], axis=0, keepdims=True)
             for h in heads], axis=1)
        for h in heads:
            sl = slice(h * M_DH, (h + 1) * M_DH)
            w_inter, mt, _, _ = per_head[h]
            blk = slice(h * c, (h + 1) * c)
            num = w_inter * inter_st[blk, 0:M_DH] + pv_st[blk, sl]
            den = w_inter * inter_st[blk, M_DH:2 * M_DH] + pv_st[blk, M_WIDTH:M_WIDTH + M_DH]
            hh = num / jnp.maximum(jnp.abs(den), jnp.exp(-mt))
            o_ref[p, rows, sl] = _head_rmsnorm(hh, gn_ref[:, sl]) * jax.nn.sigmoid(mo[:, sl])
        return new_c, new_n, [per_head[h][2] for h in heads]

    def store_state(p, idx, c_all, n_all, m_prevs):
        m_row = jnp.zeros((1, M_HEADS), f32)
        for h in heads:
            sl = slice(h * M_DH, (h + 1) * M_DH)
            cout_ref[p, idx, h] = c_all[:, sl]
            nout_ref[p, idx, pl.ds(h, 1), :] = n_all[:, sl]
            m_row = jnp.where(lane_h == h, m_prevs[h], m_row)
        mout_ref[p, idx] = m_row

    if seq.short:
        def body(ci, carry):
            rows = pl.ds(pl.multiple_of(ci * c, c), c)
            if seq.has_state:
                n0 = n0_ref[ci]
                m0 = m0_ref[ci]
                state = (jnp.concatenate([c0_ref[ci, h] for h in heads], axis=1),
                         jnp.concatenate([n0[h:h + 1, :] for h in heads], axis=1),
                         [m0[:, h:h + 1] for h in heads])
            else:
                state = (jnp.zeros((M_DH, M_WIDTH), f32), jnp.zeros((1, M_WIDTH), f32),
                         [jnp.zeros((1, 1), f32)] * M_HEADS)
            store_state(0, ci, *chunk(0, rows, *state))
            return carry

        lax.fori_loop(0, seq.rows // c, body, 0, unroll=SHORT_UNROLL)
    else:
        @pl.when(i % seq.tiles_per_seq == 0)
        def _():
            c_st[...] = jnp.zeros_like(c_st)
            n_st[...] = jnp.zeros_like(n_st)
            m_st[...] = jnp.zeros_like(m_st)

        state = [(c_st[p], n_st[p, 0:1, :], [m_st[p, h:h + 1, 0:1] for h in heads])
                 for p in groups]
        for ci in range(seq.rows // c):
            for p in groups:
                state[p] = chunk(p, pl.ds(ci * c, c), *state[p])
        for p in groups:
            c_st[p] = state[p][0]
            n_st[p, 0:1, :] = state[p][1]
            for h in heads:
                m_st[p, h:h + 1, :] = jnp.broadcast_to(state[p][2][h], (1, LANES))

        @pl.when(i % seq.tiles_per_seq == seq.tiles_per_seq - 1)
        def _():
            for p in groups:
                store_state(p, 0, *state[p])


def _mlstm(proj, cw, cb, wq, wk, wv, gn, states, layer, seq):
    m, n = proj.shape
    r = seq.rows
    par = seq.par
    proj = proj.reshape(par, m // par, n)

    def lw(shape):
        nd = len(shape)
        return pl.BlockSpec((None,) + shape, lambda i: (layer,) + (0,) * nd)

    in_specs = [seq.par_rows_spec(512, COL_MU // 512), seq.par_rows_spec(512, COL_MO // 512),
                seq.par_rows_spec(LANES, COL_MIF // LANES),
                lw((CONV_W, M_WIDTH)), lw((1, M_WIDTH)),
                lw((M_HEADS, M_DH, M_DH)), lw((M_HEADS, M_DH, M_DH)), lw((M_HEADS, M_DH, M_DH)),
                lw((1, M_WIDTH))]
    args = [proj, proj, proj, cw, cb, wq, wk, wv, gn]
    tails = [(M_HEADS, M_DH, M_DH), (M_HEADS, M_DH), (1, M_HEADS), (CONV_W - 1, M_WIDTH)]
    if seq.has_state:
        in_specs += [seq.in_state_spec(t, layer) for t in tails]
        args += list(states)
    outs = pl.pallas_call(
        functools.partial(_mlstm_kernel, seq=seq),
        grid=(seq.n_tiles,),
        in_specs=in_specs,
        out_specs=[seq.par_rows_spec(M_WIDTH, 0)] + [seq.par_state_spec(t) for t in tails],
        out_shape=[jax.ShapeDtypeStruct((par, m // par, M_WIDTH), f32)]
        + [jax.ShapeDtypeStruct((par, seq.batch // par) + t, f32) for t in tails],
        scratch_shapes=[pltpu.VMEM((par, r + SUBLANES, M_WIDTH), f32),
                        pltpu.VMEM((par, r, M_WIDTH), bf16),
                        pltpu.VMEM((par, r, M_WIDTH), f32), pltpu.VMEM((par, r, M_WIDTH), f32),
                        pltpu.VMEM((par, r, M_WIDTH), f32),
                        pltpu.VMEM((par, M_DH, M_WIDTH), f32),
                        pltpu.VMEM((par, SUBLANES, M_WIDTH), f32),
                        pltpu.VMEM((par, SUBLANES, LANES), f32)],
        compiler_params=_params(1),
        name="mlstm",
    )(*args)
    return [outs[0].reshape(m, M_WIDTH)] + [
        o.reshape((seq.batch,) + t) for o, t in zip(outs[1:], tails)]


def _rglru_kernel(*refs, seq):
    n_in = 9
    (rx_ref, ry_ref, cw_ref, cb_ref, wr_ref, br_ref, wi_ref, bi_ref, lam_ref) = refs[:n_in]
    if seq.has_state:
        h0_ref, conv0_ref = refs[n_in:n_in + 2]
        rest = refs[n_in + 2:]
    else:
        h0_ref = conv0_ref = None
        rest = refs[n_in:]
    o_ref, hout_ref, convout_ref, ucat_ref, xc_sc, a_sc, b_sc, h_st = rest
    r = seq.rows
    i = pl.program_id(0)

    _conv_carry(seq, ucat_ref)
    _conv_store(seq, ucat_ref, rx_ref[...], conv0_ref, CONV_W, slice(0, R_WIDTH))
    _conv_state_out(seq, ucat_ref, convout_ref, CONV_W)
    pieces = [(r0, slice(c0, c0 + CONV_COLS))
              for r0 in range(0, r, CONV_ROWS) for c0 in range(0, R_WIDTH, CONV_COLS)]
    for r0, cols in pieces:
        xc_sc[pl.ds(r0, CONV_ROWS), cols] = _conv_piece(ucat_ref, cw_ref, cb_ref, r0, CONV_ROWS,
                                                        cols, CONV_W)
    xc_bf = xc_sc[...].astype(bf16)
    a_sc[...] = jnp.dot(xc_bf, wr_ref[...], preferred_element_type=f32) + br_ref[...]
    b_sc[...] = jnp.dot(xc_bf, wi_ref[...], preferred_element_type=f32) + bi_ref[...]
    log_lam = _log_sigmoid(lam_ref[...])
    for r0, cols in pieces:
        rows = pl.ds(r0, CONV_ROWS)
        log_a = R_C * jax.nn.sigmoid(a_sc[rows, cols]) * log_lam[:, cols]
        a = jnp.exp(log_a)
        gated_x = jax.nn.sigmoid(b_sc[rows, cols]) * xc_sc[rows, cols]
        a_sc[rows, cols] = a
        b_sc[rows, cols] = jnp.sqrt(-jnp.tanh(log_a) * (a * a + 1.0)) * gated_x

    if not seq.short:
        @pl.when(i % seq.tiles_per_seq == 0)
        def _():
            h_st[...] = jnp.zeros_like(h_st)

    sub = lax.broadcasted_iota(jnp.int32, (SUBLANES, R_WIDTH), 0)

    def body(gi, h_prev):
        rows = pl.ds(pl.multiple_of(gi * SUBLANES, SUBLANES), SUBLANES)
        ag = a_sc[rows, :]
        bg = b_sc[rows, :]
        for d in (1, 2, 4):
            keep = sub >= d
            bg = jnp.where(keep, ag * pltpu.roll(bg, d, 0) + bg, bg)
            ag = jnp.where(keep, ag * pltpu.roll(ag, d, 0), ag)
        if seq.short:
            if seq.has_state:
                h_in = h0_ref[gi]
            else:
                h_in = jnp.zeros((1, R_WIDTH), f32)
        else:
            h_in = h_prev
        hg = bg + ag * h_in
        o_ref[rows, :] = hg * _gelu_tanh(ry_ref[rows, :])
        if seq.short:
            hout_ref[gi] = hg[seq.seq_real - 1:seq.seq_real, :]
            return h_prev
        return hg[SUBLANES - 1:SUBLANES, :]

    h_last = lax.fori_loop(0, r // SUBLANES, body, h_st[0:1, :], unroll=GROUP_UNROLL)

    if not seq.short:
        h_st[0:1, :] = h_last

        @pl.when(i % seq.tiles_per_seq == seq.tiles_per_seq - 1)
        def _():
            hout_ref[0] = h_last


def _rglru(proj, cw, cb, wr, br, wi, bi, lam, states, layer, seq):
    m = proj.shape[0]
    r = seq.rows

    def col(cb_, w):
        return pl.BlockSpec((r, w), lambda i: (i, cb_))

    def lw(shape):
        n = len(shape)
        return pl.BlockSpec((None,) + shape, lambda i: (layer,) + (0,) * n)

    in_specs = [col(COL_RX // 512, 512), col(COL_RY // 512, 512),
                lw((CONV_W, R_WIDTH)), lw((1, R_WIDTH)),
                lw((R_WIDTH, R_WIDTH)), lw((1, R_WIDTH)),
                lw((R_WIDTH, R_WIDTH)), lw((1, R_WIDTH)), lw((1, R_WIDTH))]
    args = [proj, proj, cw, cb, wr, br, wi, bi, lam]
    tails = [(1, R_WIDTH), (CONV_W - 1, R_WIDTH)]
    if seq.has_state:
        in_specs += [seq.in_state_spec(t, layer) for t in tails]
        args += list(states)
    return pl.pallas_call(
        functools.partial(_rglru_kernel, seq=seq),
        grid=(seq.n_tiles,),
        in_specs=in_specs,
        out_specs=[pl.BlockSpec((r, R_WIDTH), lambda i: (i, 0))]
        + [seq.state_spec(t) for t in tails],
        out_shape=[jax.ShapeDtypeStruct((m, R_WIDTH), f32)]
        + [jax.ShapeDtypeStruct((seq.batch,) + t, f32) for t in tails],
        scratch_shapes=[pltpu.VMEM((r + SUBLANES, R_WIDTH), f32), pltpu.VMEM((r, R_WIDTH), f32),
                        pltpu.VMEM((r, R_WIDTH), f32), pltpu.VMEM((r, R_WIDTH), f32),
                        pltpu.VMEM((SUBLANES, R_WIDTH), f32)],
        compiler_params=_params(1),
        name="rglru",
    )(*args)


def _merge_ffn_kernel(*refs, seq, final_norm):
    n_in = 13
    (gates_ref, og_ref, om_ref, or_ref, x_ref, wb_ref, wo_ref,
     g_ref, wu_ref, cw_ref, cb_ref, wd_ref, gf_ref) = refs[:n_in]
    if seq.has_state:
        conv0_ref = refs[n_in]
        rest = refs[n_in + 1:]
    else:
        conv0_ref = None
        rest = refs[n_in:]
    o_ref, convout_ref, ucat_ref, act_sc, mix_sc = rest

    branches = (og_ref, om_ref, or_ref)
    o_bf = [b_ref[...].astype(bf16) for b_ref in branches]
    for c0 in range(0, D_MODEL, CONV_COLS):
        mix = None
        for j in range(len(branches)):
            gate = jax.nn.sigmoid(gates_ref[:, j * D_MODEL + c0:j * D_MODEL + c0 + CONV_COLS])
            term = gate * jnp.dot(o_bf[j], wb_ref[j * 512:(j + 1) * 512, c0:c0 + CONV_COLS],
                                  preferred_element_type=f32)
            mix = term if mix is None else mix + term
        mix_sc[:, c0:c0 + CONV_COLS] = mix.astype(bf16)
    x = x_ref[...] + jnp.dot(mix_sc[...], wo_ref[...], preferred_element_type=f32)

    xn = (x * lax.rsqrt(jnp.mean(x * x, axis=-1, keepdims=True) + EPS) * g_ref[...]).astype(bf16)
    _conv_carry(seq, ucat_ref)
    y = x
    for g0 in range(0, D_FF, FFN_GROUP):
        g1 = min(g0 + FFN_GROUP, D_FF)
        for c0 in range(g0, g1, CONV_COLS):
            halves = [slice(base + c0, base + c0 + CONV_COLS) for base in (0, D_FF)]
            for cols in halves:
                u = jnp.dot(xn, wu_ref[:, cols], preferred_element_type=f32)
                _conv_store(seq, ucat_ref, u, conv0_ref, FFN_CONV_W, cols)
            for r0 in range(0, seq.rows, CONV_ROWS):
                for p0 in range(c0, c0 + CONV_COLS, LANES):
                    u_a, u_b = (_conv_piece(ucat_ref, cw_ref, cb_ref, r0, CONV_ROWS,
                                            slice(base + p0, base + p0 + LANES), FFN_CONV_W)
                                for base in (0, D_FF))
                    act_sc[pl.ds(r0, CONV_ROWS), p0:p0 + LANES] = (
                        _gelu_tanh(u_a) * u_b).astype(bf16)
        y = y + jnp.dot(act_sc[:, g0:g1], wd_ref[g0:g1, :], preferred_element_type=f32)
    _conv_state_out(seq, ucat_ref, convout_ref, FFN_CONV_W)
    if final_norm:
        y = y * lax.rsqrt(jnp.mean(y * y, axis=-1, keepdims=True) + EPS) * gf_ref[...]
    o_ref[...] = y


def _merge_ffn(proj, o_g, o_m, o_r, x, wts, conv0, layer, seq, final_norm):
    m = x.shape[0]
    r = seq.rows
    row = lambda w: pl.BlockSpec((r, w), lambda i: (i, 0))

    def lw(shape, **kw):
        n = len(shape)
        return pl.BlockSpec((None,) + shape, lambda i: (layer,) + (0,) * n, **kw)

    once = dict(pipeline_mode=pl.Buffered(1))
    in_specs = [row(3 * D_MODEL), row(512), row(512), row(512), row(D_MODEL),
                lw((1536, D_MODEL), **once), lw((D_MODEL, D_MODEL), **once),
                lw((1, D_MODEL)), lw((D_MODEL, 2 * D_FF), **once),
                lw((FFN_CONV_W, 2 * D_FF)), lw((1, 2 * D_FF)), lw((D_FF, D_MODEL), **once),
                pl.BlockSpec((1, D_MODEL), lambda i: (0, 0))]
    args = [proj, o_g, o_m, o_r, x, wts['w_branch'], wts['w_out'], wts['norm_ffn_g'], wts['w_up'],
            wts['ffn_conv_w'], wts['ffn_conv_b'], wts['w_down'], wts['norm_f_g']]
    tail = (FFN_CONV_W - 1, 2 * D_FF)
    if seq.has_state:
        in_specs.append(seq.in_state_spec(tail, layer))
        args.append(conv0)
    return pl.pallas_call(
        functools.partial(_merge_ffn_kernel, seq=seq, final_norm=final_norm),
        grid=(seq.n_tiles,),
        in_specs=in_specs,
        out_specs=[row(D_MODEL), seq.state_spec(tail)],
        out_shape=[jax.ShapeDtypeStruct((m, D_MODEL), f32),
                   jax.ShapeDtypeStruct((seq.batch,) + tail, f32)],
        scratch_shapes=[pltpu.VMEM((r + SUBLANES, 2 * D_FF), f32), pltpu.VMEM((r, D_FF), bf16),
                        pltpu.VMEM((r, D_MODEL), bf16)],
        compiler_params=_params(1),
        name="merge_ffn",
    )(*args)


def _head_pad(a):
    lead = a.shape[:-1]
    a = a.reshape(lead + (GLA_HEADS, GLA_DK))
    a = jnp.pad(a, [(0, 0)] * len(lead) + [(0, 0), (0, LANES - GLA_DK)])
    return a.reshape(lead + (GLA_HEADS * LANES,))


def _lane_pad(a, width):
    return jnp.pad(a, [(0, 0)] * (a.ndim - 1) + [(0, width - a.shape[-1])])


def _permute_in_cols(a):
    sizes = (256, 256, 512, 512, GLA_RANK, 512, M_HEADS, M_HEADS, 512, 512, 512, 3 * D_MODEL)
    parts = []
    off = 0
    for s in sizes:
        parts.append(a[..., off:off + s])
        off += s
    gq, gk, gv, gg, ga, mu, mi, mf, mo, rx, ry, gates = parts
    out = jnp.concatenate(
        [gates, _head_pad(gq), _head_pad(gk), gv, gg, mu, mo, rx, ry,
         _lane_pad(ga, LANES), _lane_pad(jnp.concatenate([mi, mf], axis=-1), LANES)], axis=-1)
    return _lane_pad(out, D_IN_PAD)


def _block_diag(w):
    d, nb, bs, _ = w.shape
    eye = jnp.eye(nb, dtype=w.dtype)
    return jnp.einsum('lnde,nm->lndme', w, eye).reshape(d, nb * bs, nb * bs)


def _run_group(x, seq_pair, seq_mix, seq_ffn, states, wts, tm):
    per_layer = []
    for l in range(DEPTH):
        proj = _in_proj(x, wts['norm_mix_g'], wts['w_in'], wts['b_in'], l, tm=tm, tn=2560)
        st = states
        o_g, s_new = _gla(proj, wts['w_gla_a2'], wts['b_gla_a2'], wts['gla_norm_g'],
                          st[0] if st else None, l, seq_pair)
        o_m, c_new, n_new, m_new, mconv_new = _mlstm(
            proj, wts['mlstm_conv_w'], wts['mlstm_conv_b'], wts['w_mlstm_q'], wts['w_mlstm_k'],
            wts['w_mlstm_v'], wts['mlstm_norm_g'], st[1:5] if st else None, l, seq_pair)
        o_r, h_new, rconv_new = _rglru(
            proj, wts['rglru_conv_w'], wts['rglru_conv_b'], wts['w_rglru_r'], wts['b_rglru_r'],
            wts['w_rglru_i'], wts['b_rglru_i'], wts['rglru_lambda'], st[5:7] if st else None,
            l, seq_mix)
        x, fconv_new = _merge_ffn(proj, o_g, o_m, o_r, x, wts, st[7] if st else None, l, seq_ffn,
                                  final_norm=(l == DEPTH - 1))
        per_layer.append([s_new, c_new, n_new, m_new, mconv_new, h_new, rconv_new, fconv_new])
    new_states = [jnp.stack([st[j] for st in per_layer]) for j in range(8)]
    return x, new_states


def kernel(x_prompt, x_sample, state_gla_S, state_mlstm_C, state_mlstm_n, state_mlstm_m,
           state_mlstm_conv, state_rglru_h, state_rglru_conv, state_ffn_conv,
           norm_mix_g, w_in, b_in, w_gla_a2, b_gla_a2, gla_norm_g,
           mlstm_conv_w, mlstm_conv_b, w_mlstm_q, w_mlstm_k, w_mlstm_v, mlstm_norm_g,
           rglru_conv_w, rglru_conv_b, w_rglru_r, b_rglru_r, w_rglru_i, b_rglru_i, rglru_lambda,
           w_branch, w_out, norm_ffn_g, w_up, ffn_conv_w, ffn_conv_b, w_down, norm_f_g):
    row = lambda a: a[:, None, :]
    wts = {
        'norm_mix_g': row(norm_mix_g),
        'w_in': _permute_in_cols(w_in).astype(bf16),
        'b_in': row(_permute_in_cols(b_in)),
        'w_gla_a2': _head_pad(jnp.pad(w_gla_a2, ((0, 0), (0, LANES - GLA_RANK), (0, 0)))).astype(bf16),
        'b_gla_a2': row(_head_pad(b_gla_a2)),
        'gla_norm_g': row(gla_norm_g),
        'mlstm_conv_w': mlstm_conv_w, 'mlstm_conv_b': row(mlstm_conv_b),
        'w_mlstm_q': w_mlstm_q.astype(bf16), 'w_mlstm_k': w_mlstm_k.astype(bf16),
        'w_mlstm_v': w_mlstm_v.astype(bf16), 'mlstm_norm_g': row(mlstm_norm_g),
        'rglru_conv_w': rglru_conv_w, 'rglru_conv_b': row(rglru_conv_b),
        'w_rglru_r': _block_diag(w_rglru_r).astype(bf16), 'b_rglru_r': row(b_rglru_r),
        'w_rglru_i': _block_diag(w_rglru_i).astype(bf16), 'b_rglru_i': row(b_rglru_i),
        'rglru_lambda': row(rglru_lambda),
        'w_branch': w_branch.astype(bf16), 'w_out': w_out.astype(bf16),
        'norm_ffn_g': row(norm_ffn_g), 'w_up': w_up.astype(bf16),
        'ffn_conv_w': ffn_conv_w, 'ffn_conv_b': row(ffn_conv_b),
        'w_down': w_down.astype(bf16), 'norm_f_g': norm_f_g[None, :],
    }

    bp, lp, _ = x_prompt.shape
    seq_p_pair = _Seq(bp, lp, lp, 512, has_state=False, par=2)
    seq_p_mix = _Seq(bp, lp, lp, 512, has_state=False)
    seq_p_ffn = _Seq(bp, lp, lp, 256, has_state=False)
    y_p, p_states = _run_group(x_prompt.reshape(bp * lp, D_MODEL), seq_p_pair, seq_p_mix,
                               seq_p_ffn, None, wts, tm=1024)
    y_prompt = y_p.reshape(bp, lp, D_MODEL)

    bs, ls, _ = x_sample.shape
    xs = jnp.pad(x_sample, ((0, 0), (0, SUBLANES - ls), (0, 0))).reshape(bs * SUBLANES, D_MODEL)
    seq_s = _Seq(bs, SUBLANES, ls, 128, has_state=True)
    s_in = [state_gla_S, state_mlstm_C, state_mlstm_n, state_mlstm_m[:, :, None, :],
            state_mlstm_conv, state_rglru_h[:, :, None, :], state_rglru_conv, state_ffn_conv]
    y_s, s_states = _run_group(xs, seq_s, seq_s, seq_s, s_in, wts, tm=1024)
    y_sample = y_s.reshape(bs, SUBLANES, D_MODEL)[:, :ls]

    def unpack(st):
        s, c, n, m, mconv, h, rconv, fconv = st
        return (s, c, n, m[:, :, 0, :], mconv, h[:, :, 0, :], rconv, fconv)

    return (y_prompt, y_sample) + unpack(p_states) + unpack(s_states)
```
